```python
import jax, jax.numpy as jnp
from jax import lax
import numpy as np

D_MODEL = 2048
BATCH = 4
SEQ = 2048
DEPTH = 2
DEC_BATCH = 8
DEC_SEQ = 4096
PAST_LEN = 128

HEAD_DIM = 128
N_ATT_HEADS = 12
ATT_W = N_ATT_HEADS * HEAD_DIM
CONV_W = D_MODEL - ATT_W
CONV_K = 3
W_IN_COLS = 3 * ATT_W + 3 * CONV_W
DILATED_PAIRS = ((128, 1), (512, 4), (2048, 16))
BLK = 64
ROT_DIM = HEAD_DIM // 4
ROPE_THETA = 500000.0
N_MEM = 256
X_HEADS = 4
X_HEAD_DIM = D_MODEL // X_HEADS
D_FF = ((8 * D_MODEL // 3 + 255) // 256) * 256
EPS = 1e-6
NEG = -1e30

kernel_name = "hymba_dilated_conv_encoder"


def _rms(x, g):
    xf = x.astype(jnp.float32)
    y = xf * lax.rsqrt(jnp.mean(xf * xf, axis=-1, keepdims=True) + EPS)
    return (y * g.astype(jnp.float32)).astype(x.dtype)


def _rope_partial(t):
    S = t.shape[1]
    half = ROT_DIM // 2
    inv = jnp.float32(ROPE_THETA) ** (-jnp.arange(0, ROT_DIM, 2, dtype=jnp.float32) / ROT_DIM)
    ang = jnp.arange(S, dtype=jnp.float32)[:, None] * inv[None, :]
    cos = jnp.cos(ang)[None, :, None, :]
    sin = jnp.sin(ang)[None, :, None, :]
    tr = t[..., :ROT_DIM].astype(jnp.float32)
    t1, t2 = tr[..., :half], tr[..., half:]
    rot = jnp.concatenate([t1 * cos - t2 * sin, t2 * cos + t1 * sin], axis=-1).astype(t.dtype)
    return jnp.concatenate([rot, t[..., ROT_DIM:]], axis=-1)


def _dilated_branch(q, k, v, dil, half):
    B, S, H, dh = q.shape
    L = S // dil
    nb = -(-L // BLK)
    Lp = nb * BLK
    G = B * dil

    def to_res(t):
        return t.reshape(B, L, dil, H, dh).transpose(0, 2, 1, 3, 4).reshape(G, L, H, dh)

    qr = jnp.pad(to_res(q), ((0, 0), (0, Lp - L), (0, 0), (0, 0)))
    kpad = ((0, 0), (BLK, Lp - L + BLK), (0, 0), (0, 0))
    kr = jnp.pad(to_res(k), kpad)
    vr = jnp.pad(to_res(v), kpad)
    scale = dh ** -0.5

    def block(n):
        start = n * BLK
        qb = lax.dynamic_slice_in_dim(qr, start, BLK, axis=1)
        kb = lax.dynamic_slice_in_dim(kr, start, 3 * BLK, axis=1)
        vb = lax.dynamic_slice_in_dim(vr, start, 3 * BLK, axis=1)
        s = jnp.einsum('gqhd,gkhd->ghqk', qb, kb, preferred_element_type=jnp.float32) * scale
        qpos = start + jnp.arange(BLK)
        kpos = start - BLK + jnp.arange(3 * BLK)
        valid = ((jnp.abs(qpos[:, None] - kpos[None, :]) <= half)
                 & (kpos >= 0)[None, :] & (kpos < L)[None, :])
        s = jnp.where(valid, s, NEG)
        m = jnp.max(s, axis=-1, keepdims=True)
        p = jnp.exp(s - m)
        l = jnp.sum(p, axis=-1, keepdims=True)
        o = jnp.einsum('ghqk,gkhd->gqhd', p.astype(vb.dtype), vb,
                       preferred_element_type=jnp.float32)
        o = o / jnp.swapaxes(l[..., 0], 1, 2)[..., None]
        lse = jnp.swapaxes((m + jnp.log(l))[..., 0], 1, 2)
        return o, lse

    o, lse = lax.map(block, jnp.arange(nb))
    o = o.transpose(1, 0, 2, 3, 4).reshape(G, Lp, H, dh)[:, :L]
    lse = lse.transpose(1, 0, 2, 3).reshape(G, Lp, H)[:, :L]
    o = o.reshape(B, dil, L, H, dh).transpose(0, 2, 1, 3, 4).reshape(B, S, H, dh)
    lse = lse.reshape(B, dil, L, H).transpose(0, 2, 1, 3).reshape(B, S, H)
    return o, lse


def _dilated_attention(q, k, v):
    outs, lses = [], []
    for window, dil in DILATED_PAIRS:
        o, lse = _dilated_branch(q, k, v, dil, window // (2 * dil))
        outs.append(o)
        lses.append(lse)
    w = jax.nn.softmax(jnp.stack(lses, axis=0), axis=0)
    o = jnp.sum(w[..., None] * jnp.stack(outs, axis=0), axis=0)
    return o.astype(q.dtype)


def _short_conv(b, c, h, conv_w):
    u = c * h
    up = jnp.pad(u, ((0, 0), (1, 1), (0, 0)))
    y = up[:, :-2] * conv_w[0] + up[:, 1:-1] * conv_w[1] + up[:, 2:] * conv_w[2]
    return b * y


def _cross_attn(x, mem, g_mem, w_q, w_k, w_v, w_o):
    B, S, _ = x.shape
    M = mem.shape[1]
    mn = _rms(mem, g_mem)
    q = (x @ w_q).reshape(B, S, X_HEADS, X_HEAD_DIM)
    k = (mn @ w_k).reshape(B, M, X_HEADS, X_HEAD_DIM)
    v = (mn @ w_v).reshape(B, M, X_HEADS, X_HEAD_DIM)
    s = jnp.einsum('bshd,bmhd->bhsm', q, k, preferred_element_type=jnp.float32) * (X_HEAD_DIM ** -0.5)
    p = jax.nn.softmax(s, axis=-1)
    o = jnp.einsum('bhsm,bmhd->bshd', p.astype(v.dtype), v).reshape(B, S, D_MODEL)
    return o @ w_o


def _layer(x, mem, g_mix_pre, w_in, conv_w, g_attn_out, g_conv_out, w_o, g_mix_post,
           g_x_pre, g_mem, w_xq, w_xk, w_xv, w_xo, g_x_post,
           g_ffn_pre, w_gate, w_up, w_down, g_ffn_post):
    B, S, _ = x.shape
    h = _rms(x, g_mix_pre)
    z = h @ w_in
    q, k, v, gb, gc, gh = jnp.split(
        z, np.cumsum([ATT_W, ATT_W, ATT_W, CONV_W, CONV_W]).tolist(), axis=-1)
    q = _rope_partial(q.reshape(B, S, N_ATT_HEADS, HEAD_DIM))
    k = _rope_partial(k.reshape(B, S, N_ATT_HEADS, HEAD_DIM))
    v = v.reshape(B, S, N_ATT_HEADS, HEAD_DIM)
    att = _dilated_attention(q, k, v).reshape(B, S, ATT_W)
    cnv = _short_conv(gb, gc, gh, conv_w)
    mix = jnp.concatenate([_rms(att, g_attn_out), _rms(cnv, g_conv_out)], axis=-1) @ w_o
    x = x + _rms(mix, g_mix_post)
    xa = _cross_attn(_rms(x, g_x_pre), mem, g_mem, w_xq, w_xk, w_xv, w_xo)
    x = x + _rms(xa, g_x_post)
    hf = _rms(x, g_ffn_pre)
    f = (jax.nn.silu(hf @ w_gate) * (hf @ w_up)) @ w_down
    x = x + _rms(f, g_ffn_post)
    return x


def _trunk(x, mem, g_mix_pre, w_in, conv_w, g_attn_out, g_conv_out, w_o, g_mix_post,
           g_x_pre, g_mem, w_xq, w_xk, w_xv, w_xo, g_x_post,
           g_ffn_pre, w_gate, w_up, w_down, g_ffn_post):
    for l in range(DEPTH):
        x = _layer(x, mem, g_mix_pre[l], w_in[l], conv_w[l], g_attn_out[l], g_conv_out[l],
                   w_o[l], g_mix_post[l], g_x_pre[l], g_mem[l], w_xq[l], w_xk[l], w_xv[l],
                   w_xo[l], g_x_post[l], g_ffn_pre[l], w_gate[l], w_up[l], w_down[l],
                   g_ffn_post[l])
    return x


def setup_inputs(seed: int = 0) -> dict:
    key = jax.random.key(seed)
    ks = jax.random.split(key, 24)
    f32 = jnp.float32

    def nrm(k, shape, fan_in):
        return jax.random.normal(k, shape, f32) * (fan_in ** -0.5)

    def gain(k, n):
        return 1.0 + 0.02 * jax.random.normal(k, (DEPTH, n), f32)

    return {
        "x_prompt": jax.random.normal(ks[0], (BATCH, SEQ, D_MODEL), f32),
        "x_sample": jax.random.normal(ks[1], (DEC_BATCH, DEC_SEQ, D_MODEL), f32),
        "mem_prompt": jax.random.normal(ks[2], (BATCH, N_MEM, D_MODEL), f32),
        "mem_sample": jax.random.normal(ks[3], (DEC_BATCH, N_MEM, D_MODEL), f32),
        "g_mix_pre": gain(ks[4], D_MODEL),
        "w_in": nrm(ks[5], (DEPTH, D_MODEL, W_IN_COLS), D_MODEL),
        "conv_w": nrm(ks[6], (DEPTH, CONV_K, CONV_W), CONV_K),
        "g_attn_out": gain(ks[7], ATT_W),
        "g_conv_out": gain(ks[8], CONV_W),
        "w_o": nrm(ks[9], (DEPTH, D_MODEL, D_MODEL), D_MODEL),
        "g_mix_post": gain(ks[10], D_MODEL),
        "g_x_pre": gain(ks[11], D_MODEL),
        "g_mem": gain(ks[12], D_MODEL),
        "w_xq": nrm(ks[13], (DEPTH, D_MODEL, D_MODEL), D_MODEL),
        "w_xk": nrm(ks[14], (DEPTH, D_MODEL, D_MODEL), D_MODEL),
        "w_xv": nrm(ks[15], (DEPTH, D_MODEL, D_MODEL), D_MODEL),
        "w_xo": nrm(ks[16], (DEPTH, D_MODEL, D_MODEL), D_MODEL),
        "g_x_post": gain(ks[17], D_MODEL),
        "g_ffn_pre": gain(ks[18], D_MODEL),
        "w_gate": nrm(ks[19], (DEPTH, D_MODEL, D_FF), D_MODEL),
        "w_up": nrm(ks[20], (DEPTH, D_MODEL, D_FF), D_MODEL),
        "w_down": nrm(ks[21], (DEPTH, D_FF, D_MODEL), D_FF),
        "g_ffn_post": gain(ks[22], D_MODEL),
    }


def reference(x_prompt, x_sample, mem_prompt, mem_sample, g_mix_pre, w_in, conv_w,
              g_attn_out, g_conv_out, w_o, g_mix_post, g_x_pre, g_mem, w_xq, w_xk, w_xv,
              w_xo, g_x_post, g_ffn_pre, w_gate, w_up, w_down, g_ffn_post):
    y_prompt = _trunk(x_prompt, mem_prompt, g_mix_pre, w_in, conv_w, g_attn_out, g_conv_out,
                      w_o, g_mix_post, g_x_pre, g_mem, w_xq, w_xk, w_xv, w_xo, g_x_post,
                      g_ffn_pre, w_gate, w_up, w_down, g_ffn_post)
    y_sample = _trunk(x_sample, mem_sample, g_mix_pre, w_in, conv_w, g_attn_out, g_conv_out,
                      w_o, g_mix_post, g_x_pre, g_mem, w_xq, w_xk, w_xv, w_xo, g_x_post,
                      g_ffn_pre, w_gate, w_up, w_down, g_ffn_post)
    return (y_prompt, y_sample)
```

```python
import functools

import jax
import jax.numpy as jnp
from jax import lax
from jax.experimental import pallas as pl
from jax.experimental.pallas import tpu as pltpu

F32 = jnp.float32
BF16 = jnp.bfloat16

D_MODEL = 2048
HEAD_DIM = 128
N_ATT_HEADS = 12
ATT_W = N_ATT_HEADS * HEAD_DIM
CONV_W = D_MODEL - ATT_W
W_IN_COLS = 3 * ATT_W + 3 * CONV_W
DILATED_PAIRS = ((128, 1), (512, 4), (2048, 16))
BAND_HALF = 64
ROT_DIM = HEAD_DIM // 4
ROPE_THETA = 500000.0
N_MEM = 256
X_HEADS = 4
X_HEAD_DIM = D_MODEL // X_HEADS
D_FF = 5632
EPS = 1e-6
NEG = -1e30

V7X_LANES = 128
V7X_BF16_SUBLANES = 16
V7X_VMEM_BYTES = 64 * 1024 * 1024
MIB = 1024 * 1024

ROW_TILE = 512
IN_PROJ_COL_TILE = 512
FFN_COL_TILE = 512
ATT_Q_TILE = 128
CONV_HALO_ROWS = V7X_BF16_SUBLANES


def _params(semantics, vmem_mib):
    assert vmem_mib * MIB < V7X_VMEM_BYTES
    return pltpu.CompilerParams(dimension_semantics=semantics,
                                vmem_limit_bytes=vmem_mib * MIB)


def _resident(shape, index_map):
    return pl.BlockSpec(shape, index_map, pipeline_mode=pl.Buffered(1))


def _rms(x, g):
    ms = jnp.mean(x * x, axis=-1, keepdims=True)
    return (x * lax.rsqrt(ms + EPS)) * g


def _norm_matmul_kernel(x_ref, g_ref, w_ref, *rest, n_rope_tiles):
    if n_rope_tiles:
        cos_ref, sin_ref, o_ref, h_ref = rest
    else:
        o_ref, h_ref = rest
    j = pl.program_id(1)

    @pl.when(j == 0)
    def _():
        h_ref[...] = _rms(x_ref[...], g_ref[...]).astype(BF16)

    acc = jnp.dot(h_ref[...], w_ref[...], preferred_element_type=F32)

    if n_rope_tiles:
        @pl.when(j < n_rope_tiles)
        def _():
            cos = cos_ref[...]
            sin = sin_ref[...]
            lane = lax.broadcasted_iota(jnp.int32, cos.shape, 1)
            first_half = lane < ROT_DIM // 2
            for c in range(acc.shape[1] // HEAD_DIM):
                t = acc[:, c * HEAD_DIM:(c + 1) * HEAD_DIM]
                hi = pltpu.roll(t, HEAD_DIM - ROT_DIM // 2, axis=1)
                lo = pltpu.roll(t, ROT_DIM // 2, axis=1)
                r = t * cos + jnp.where(first_half, hi, lo) * sin
                o_ref[:, c * HEAD_DIM:(c + 1) * HEAD_DIM] = r.astype(BF16)

        @pl.when(j >= n_rope_tiles)
        def _():
            o_ref[...] = acc.astype(BF16)
    else:
        o_ref[...] = acc.astype(BF16)


def _norm_matmul(x, g, w, *, tm, tn, rope=None, rope_cols=0, seq_len=None):
    rows, d = x.shape
    n = w.shape[1]
    n_rope_tiles = rope_cols // tn
    assert rows % tm == 0 and n % tn == 0 and rope_cols % tn == 0
    in_specs = [
        pl.BlockSpec((tm, d), lambda i, j: (i, 0)),
        pl.BlockSpec((1, d), lambda i, j: (0, 0)),
        pl.BlockSpec((d, tn), lambda i, j: (0, j)),
    ]
    args = [x, g, w]
    if n_rope_tiles:
        tiles_per_seq = seq_len // tm
        assert seq_len % tm == 0
        spec = pl.BlockSpec((tm, HEAD_DIM), lambda i, j: (i % tiles_per_seq, 0))
        in_specs += [spec, spec]
        args += list(rope)
    return pl.pallas_call(
        functools.partial(_norm_matmul_kernel, n_rope_tiles=n_rope_tiles),
        grid=(rows // tm, n // tn),
        in_specs=in_specs,
        out_specs=pl.BlockSpec((tm, tn), lambda i, j: (i, j)),
        out_shape=jax.ShapeDtypeStruct((rows, n), BF16),
        scratch_shapes=[pltpu.VMEM((tm, d), BF16)],
        compiler_params=_params(("parallel", "arbitrary"), 40),
        name="norm_matmul_rope" if n_rope_tiles else "norm_matmul",
    )(*args)


def _rope_tables(seq_len):
    half = ROT_DIM // 2
    inv = jnp.float32(ROPE_THETA) ** (-jnp.arange(0, ROT_DIM, 2, dtype=F32) / ROT_DIM)
    ang = jnp.arange(seq_len, dtype=F32)[:, None] * inv[None, :]
    cos, sin = jnp.cos(ang), jnp.sin(ang)
    pad = HEAD_DIM - ROT_DIM
    cos_t = jnp.concatenate([cos, cos, jnp.ones((seq_len, pad), F32)], axis=1)
    sin_t = jnp.concatenate([-sin, sin, jnp.zeros((seq_len, pad), F32)], axis=1)
    assert half * 2 == ROT_DIM
    return cos_t, sin_t


def _branch_kernel(q_ref, kp_ref, k_ref, kn_ref, vp_ref, v_ref, vn_ref, o_ref, lse_ref,
                   *, res_len):
    tq = q_ref.shape[0]
    base = pl.program_id(2) * tq
    keys = jnp.concatenate([kp_ref[...], k_ref[...], kn_ref[...]], axis=0)
    vals = jnp.concatenate([vp_ref[...], v_ref[...], vn_ref[...]], axis=0)
    nk = keys.shape[0]
    qi = lax.broadcasted_iota(jnp.int32, (tq, nk), 0)
    kj = lax.broadcasted_iota(jnp.int32, (tq, nk), 1)
    kpos = base - BAND_HALF + kj
    off = kj - qi
    valid = (off >= 0) & (off <= 2 * BAND_HALF) & (kpos >= 0) & (kpos < res_len)
    lane = lax.broadcasted_iota(jnp.int32, (tq, V7X_LANES), 1)
    lse_all = jnp.zeros((tq, V7X_LANES), F32)
    scale = HEAD_DIM ** -0.5
    for h in range(N_ATT_HEADS):
        cols = slice(h * HEAD_DIM, (h + 1) * HEAD_DIM)
        s = lax.dot_general(q_ref[:, cols], keys[:, cols], (((1,), (1,)), ((), ())),
                            preferred_element_type=F32) * scale
        s = jnp.where(valid, s, NEG)
        m = jnp.max(s, axis=-1, keepdims=True)
        p = jnp.exp(s - m)
        l = jnp.sum(p, axis=-1, keepdims=True)
        o = jnp.dot(p.astype(BF16), vals[:, cols], preferred_element_type=F32)
        o_ref[:, cols] = (o / l).astype(BF16)
        lse_all = jnp.where(lane == h, m + jnp.log(l), lse_all)
    lse_ref[...] = lse_all


def _dilated_branch(z, batch, seq_len, dil):
    res_len = seq_len // dil
    tq = ATT_Q_TILE
    assert res_len % tq == 0 and tq % BAND_HALF == 0
    halo_per_tile = tq // BAND_HALF
    n_halo_blocks = res_len // BAND_HALF
    zv = z.reshape(batch, res_len, dil * W_IN_COLS)
    blocks_per_res = W_IN_COLS // ATT_W

    def main(part):
        return pl.BlockSpec((None, tq, ATT_W),
                            lambda b, r, c: (b, c, r * blocks_per_res + part))

    def prev(part):
        return pl.BlockSpec((None, BAND_HALF, ATT_W),
                            lambda b, r, c: (b, jnp.maximum(c * halo_per_tile - 1, 0),
                                             r * blocks_per_res + part))

    def nxt(part):
        return pl.BlockSpec((None, BAND_HALF, ATT_W),
                            lambda b, r, c: (b, jnp.minimum((c + 1) * halo_per_tile,
                                                            n_halo_blocks - 1),
                                             r * blocks_per_res + part))

    o, lse = pl.pallas_call(
        functools.partial(_branch_kernel, res_len=res_len),
        grid=(batch, dil, res_len // tq),
        in_specs=[main(0), prev(1), main(1), nxt(1), prev(2), main(2), nxt(2)],
        out_specs=[pl.BlockSpec((None, tq, ATT_W), lambda b, r, c: (b, c, r)),
                   pl.BlockSpec((None, tq, V7X_LANES), lambda b, r, c: (b, c, r))],
        out_shape=[jax.ShapeDtypeStruct((batch, res_len, dil * ATT_W), BF16),
                   jax.ShapeDtypeStruct((batch, res_len, dil * V7X_LANES), F32)],
        compiler_params=_params(("parallel", "parallel", "parallel"), 32),
        name=f"dilated_branch_d{dil}",
    )(zv, zv, zv, zv, zv, zv, zv)
    t = batch * seq_len
    return o.reshape(t, ATT_W), lse.reshape(t, V7X_LANES)


def _mixer_out_kernel(o1_ref, o2_ref, o3_ref, l1_ref, l2_ref, l3_ref,
                      gb_ref, gc_ref, gh_ref, cp_ref, hp_ref, cn_ref, hn_ref,
                      cw_ref, ga_ref, gcv_ref, wo_ref, gpost_ref, x_ref, out_ref,
                      att_ref, *, tiles_per_seq):
    i = pl.program_id(0)
    tm = x_ref.shape[0]
    l1, l2, l3 = l1_ref[...], l2_ref[...], l3_ref[...]
    mx = jnp.maximum(jnp.maximum(l1, l2), l3)
    e1, e2, e3 = jnp.exp(l1 - mx), jnp.exp(l2 - mx), jnp.exp(l3 - mx)
    den = e1 + e2 + e3
    w1, w2, w3 = e1 / den, e2 / den, e3 / den
    ssq = jnp.zeros((tm, 1), F32)
    for h in range(N_ATT_HEADS):
        cols = slice(h * HEAD_DIM, (h + 1) * HEAD_DIM)
        a = (w1[:, h:h + 1] * o1_ref[:, cols].astype(F32)
             + w2[:, h:h + 1] * o2_ref[:, cols].astype(F32)
             + w3[:, h:h + 1] * o3_ref[:, cols].astype(F32))
        att_ref[:, cols] = a
        ssq = ssq + jnp.sum(a * a, axis=-1, keepdims=True)
    att_n = (att_ref[...] * lax.rsqrt(ssq / ATT_W + EPS)) * ga_ref[...]

    u = gc_ref[...].astype(F32) * gh_ref[...].astype(F32)
    last = CONV_HALO_ROWS - 1
    u_prev = cp_ref[last:, :].astype(F32) * hp_ref[last:, :].astype(F32)
    u_next = cn_ref[:1, :].astype(F32) * hn_ref[:1, :].astype(F32)
    pos = i % tiles_per_seq
    u_prev = jnp.where(pos == 0, jnp.zeros_like(u_prev), u_prev)
    u_next = jnp.where(pos == tiles_per_seq - 1, jnp.zeros_like(u_next), u_next)
    row = lax.broadcasted_iota(jnp.int32, u.shape, 0)
    up = jnp.where(row == 0, u_prev, pltpu.roll(u, 1, axis=0))
    dn = jnp.where(row == tm - 1, u_next, pltpu.roll(u, tm - 1, axis=0))
    cw = cw_ref[...]
    y = up * cw[0:1, :] + u * cw[1:2, :] + dn * cw[2:3, :]
    cnv = gb_ref[...].astype(F32) * y
    cnv_n = _rms(cnv, gcv_ref[...])

    mixed = jnp.concatenate([att_n.astype(BF16), cnv_n.astype(BF16)], axis=-1)
    mix = jnp.dot(mixed, wo_ref[...], preferred_element_type=F32)
    out_ref[...] = x_ref[...] + _rms(mix, gpost_ref[...])


def _mixer_out(x, z, branches, conv_w, g_attn, g_conv, w_o, g_post, *, seq_len, tm):
    t = x.shape[0]
    assert t % tm == 0 and seq_len % tm == 0 and tm % CONV_HALO_ROWS == 0
    tiles_per_seq = seq_len // tm
    halo_per_tile = tm // CONV_HALO_ROWS
    n_halo = t // CONV_HALO_ROWS
    conv_blk = 3 * ATT_W // CONV_W

    def rows(width):
        return pl.BlockSpec((tm, width), lambda i: (i, 0))

    def gate(k):
        return pl.BlockSpec((tm, CONV_W), lambda i: (i, conv_blk + k))

    def halo_prev(k):
        return pl.BlockSpec((CONV_HALO_ROWS, CONV_W),
                            lambda i: (jnp.maximum(i * halo_per_tile - 1, 0), conv_blk + k))

    def halo_next(k):
        return pl.BlockSpec((CONV_HALO_ROWS, CONV_W),
                            lambda i: (jnp.minimum((i + 1) * halo_per_tile, n_halo - 1),
                                       conv_blk + k))

    def vec(width, nrows=1):
        return pl.BlockSpec((nrows, width), lambda i: (0, 0))

    (o1, l1), (o2, l2), (o3, l3) = branches
    return pl.pallas_call(
        functools.partial(_mixer_out_kernel, tiles_per_seq=tiles_per_seq),
        grid=(t // tm,),
        in_specs=[rows(ATT_W), rows(ATT_W), rows(ATT_W),
                  rows(V7X_LANES), rows(V7X_LANES), rows(V7X_LANES),
                  gate(0), gate(1), gate(2),
                  halo_prev(1), halo_prev(2), halo_next(1), halo_next(2),
                  vec(CONV_W, 3), vec(ATT_W), vec(CONV_W),
                  _resident((D_MODEL, D_MODEL), lambda i: (0, 0)),
                  vec(D_MODEL), rows(D_MODEL)],
        out_specs=rows(D_MODEL),
        out_shape=jax.ShapeDtypeStruct((t, D_MODEL), F32),
        scratch_shapes=[pltpu.VMEM((tm, ATT_W), F32)],
        compiler_params=_params(("parallel",), 48),
        name="mixer_out",
    )(o1, o2, o3, l1, l2, l3, z, z, z, z, z, z, z,
      conv_w, g_attn, g_conv, w_o, g_post, x)


def _cross_attn_kernel(x_ref, gpre_ref, wq_ref, k_ref, v_ref, wo_ref, gpost_ref, out_ref):
    x = x_ref[...]
    xn = _rms(x, gpre_ref[...]).astype(BF16)
    q = jnp.dot(xn, wq_ref[...], preferred_element_type=F32).astype(BF16)
    scale = X_HEAD_DIM ** -0.5
    heads = []
    for h in range(X_HEADS):
        cols = slice(h * X_HEAD_DIM, (h + 1) * X_HEAD_DIM)
        s = lax.dot_general(q[:, cols], k_ref[:, cols], (((1,), (1,)), ((), ())),
                            preferred_element_type=F32) * scale
        m = jnp.max(s, axis=-1, keepdims=True)
        e = jnp.exp(s - m)
        p = e / jnp.sum(e, axis=-1, keepdims=True)
        o = jnp.dot(p.astype(BF16), v_ref[:, cols], preferred_element_type=F32)
        heads.append(o.astype(BF16))
    o = jnp.concatenate(heads, axis=-1)
    xa = jnp.dot(o, wo_ref[...], preferred_element_type=F32)
    out_ref[...] = x + _rms(xa, gpost_ref[...])


def _cross_attn(x, kv, g_pre, w_q, w_o, g_post, *, seq_len, tm):
    t = x.shape[0]
    assert t % tm == 0 and seq_len % tm == 0
    tiles_per_seq = seq_len // tm
    vec = pl.BlockSpec((1, D_MODEL), lambda i: (0, 0))
    rows = pl.BlockSpec((tm, D_MODEL), lambda i: (i, 0))
    return pl.pallas_call(
        _cross_attn_kernel,
        grid=(t // tm,),
        in_specs=[rows, vec,
                  _resident((D_MODEL, D_MODEL), lambda i: (0, 0)),
                  pl.BlockSpec((N_MEM, D_MODEL), lambda i: (i // tiles_per_seq, 0)),
                  pl.BlockSpec((N_MEM, D_MODEL), lambda i: (i // tiles_per_seq, 1)),
                  _resident((D_MODEL, D_MODEL), lambda i: (0, 0)),
                  vec],
        out_specs=rows,
        out_shape=jax.ShapeDtypeStruct((t, D_MODEL), F32),
        compiler_params=_params(("parallel",), 56),
        name="cross_attn",
    )(x, g_pre, w_q, kv, kv, w_o, g_post)


def _ffn_kernel(x_ref, gpre_ref, wg_ref, wu_ref, wd_ref, gpost_ref, out_ref, h_ref):
    j = pl.program_id(1)

    @pl.when(j == 0)
    def _():
        h_ref[...] = _rms(x_ref[...], gpre_ref[...]).astype(BF16)

    h = h_ref[...]
    gate = jnp.dot(h, wg_ref[...], preferred_element_type=F32)
    up = jnp.dot(h, wu_ref[...], preferred_element_type=F32)
    act = (gate * jax.nn.sigmoid(gate) * up).astype(BF16)
    part = jnp.dot(act, wd_ref[...], preferred_element_type=F32)

    @pl.when(j == 0)
    def _():
        out_ref[...] = part

    @pl.when(j > 0)
    def _():
        out_ref[...] += part

    @pl.when(j == pl.num_programs(1) - 1)
    def _():
        out_ref[...] = x_ref[...] + _rms(out_ref[...], gpost_ref[...])


def _ffn(x, g_pre, w_gate, w_up, w_down, g_post, *, tm, tf):
    t = x.shape[0]
    assert t % tm == 0 and D_FF % tf == 0
    vec = pl.BlockSpec((1, D_MODEL), lambda i, j: (0, 0))
    rows = pl.BlockSpec((tm, D_MODEL), lambda i, j: (i, 0))
    return pl.pallas_call(
        _ffn_kernel,
        grid=(t // tm, D_FF // tf),
        in_specs=[rows, vec,
                  pl.BlockSpec((D_MODEL, tf), lambda i, j: (0, j)),
                  pl.BlockSpec((D_MODEL, tf), lambda i, j: (0, j)),
                  pl.BlockSpec((tf, D_MODEL), lambda i, j: (j, 0)),
                  vec],
        out_specs=rows,
        out_shape=jax.ShapeDtypeStruct((t, D_MODEL), F32),
        scratch_shapes=[pltpu.VMEM((tm, D_MODEL), BF16)],
        compiler_params=_params(("parallel", "arbitrary"), 48),
        name="swiglu_ffn",
    )(x, g_pre, w_gate, w_up, w_down, g_post)


def _trunk(x, mem, layers, rope):
    batch, seq_len, _ = x.shape
    x = x.reshape(batch * seq_len, D_MODEL)
    mem = mem.reshape(batch * N_MEM, D_MODEL)
    for p in layers:
        z = _norm_matmul(x, p["g_mix_pre"], p["w_in"], tm=ROW_TILE, tn=IN_PROJ_COL_TILE,
                         rope=rope, rope_cols=2 * ATT_W, seq_len=seq_len)
        branches = [_dilated_branch(z, batch, seq_len, dil) for _, dil in DILATED_PAIRS]
        x = _mixer_out(x, z, branches, p["conv_w"], p["g_attn_out"], p["g_conv_out"],
                       p["w_o"], p["g_mix_post"], seq_len=seq_len, tm=ROW_TILE)
        kv = _norm_matmul(mem, p["g_mem"], p["w_xkv"], tm=N_MEM, tn=IN_PROJ_COL_TILE)
        x = _cross_attn(x, kv, p["g_x_pre"], p["w_xq"], p["w_xo"], p["g_x_post"],
                        seq_len=seq_len, tm=ROW_TILE)
        x = _ffn(x, p["g_ffn_pre"], p["w_gate"], p["w_up"], p["w_down"], p["g_ffn_post"],
                 tm=ROW_TILE, tf=FFN_COL_TILE)
    return x.reshape(batch, seq_len, D_MODEL)


def kernel(x_prompt, x_sample, mem_prompt, mem_sample, g_mix_pre, w_in, conv_w, g_attn_out, g_conv_out, w_o, g_mix_post, g_x_pre, g_mem, w_xq, w_xk, w_xv, w_xo, g_x_post, g_ffn_pre, w_gate, w_up, w_down, g_ffn_post):
    depth = w_in.shape[0]
    layers = []
    for l in range(depth):
        layers.append({
            "g_mix_pre": g_mix_pre[l][None, :], "w_in": w_in[l].astype(BF16),
            "conv_w": conv_w[l], "g_attn_out": g_attn_out[l][None, :],
            "g_conv_out": g_conv_out[l][None, :], "w_o": w_o[l].astype(BF16),
            "g_mix_post": g_mix_post[l][None, :], "g_x_pre": g_x_pre[l][None, :],
            "g_mem": g_mem[l][None, :], "w_xq": w_xq[l].astype(BF16),
            "w_xkv": jnp.concatenate([w_xk[l], w_xv[l]], axis=1).astype(BF16),
            "w_xo": w_xo[l].astype(BF16), "g_x_post": g_x_post[l][None, :],
            "g_ffn_pre": g_ffn_pre[l][None, :], "w_gate": w_gate[l].astype(BF16),
            "w_up": w_up[l].astype(BF16), "w_down": w_down[l].astype(BF16),
            "g_ffn_post": g_ffn_post[l][None, :],
        })
    outs = []
    for x, mem in ((x_prompt, mem_prompt), (x_sample, mem_sample)):
        rope = _rope_tables(x.shape[1])
        outs.append(_trunk(x, mem, layers, rope))
    return tuple(outs)
```

```python
import functools

import jax
import jax.numpy as jnp
from jax import lax
from jax.experimental import pallas as pl
from jax.experimental.pallas import tpu as pltpu

F32 = jnp.float32
BF16 = jnp.bfloat16

D_MODEL = 2048
HEAD_DIM = 128
N_ATT_HEADS = 12
ATT_W = N_ATT_HEADS * HEAD_DIM
QKV_W = 3 * ATT_W
CONV_W = D_MODEL - ATT_W
GATES_W = 3 * CONV_W
DILATIONS = (1, 4, 16)
BAND_HALF = 64
ROT_DIM = HEAD_DIM // 4
ROPE_THETA = 500000.0
N_MEM = 256
X_HEADS = 4
X_HEAD_DIM = D_MODEL // X_HEADS
D_FF = 5632
EPS = 1e-6
NEG = -1e30

V7X_LANES = 128
V7X_BF16_SUBLANES = 16
V7X_VMEM_BYTES = 64 * 1024 * 1024
MIB = 1024 * 1024

ROW_TILE = 512
COL_TILE = 512
FFN_COL_TILE = 512
ATT_Q_BLOCK = 128
ATT_MAX_Q_TILE = 256
CONV_HALO_ROWS = V7X_BF16_SUBLANES
HEAD_PAIRS = N_ATT_HEADS // 2


def _params(semantics, vmem_mib):
    assert vmem_mib * MIB < V7X_VMEM_BYTES
    return pltpu.CompilerParams(dimension_semantics=semantics,
                                vmem_limit_bytes=vmem_mib * MIB)


def _resident(shape, index_map):
    return pl.BlockSpec(shape, index_map, pipeline_mode=pl.Buffered(1))


def _rms(x, g):
    ms = jnp.mean(x * x, axis=-1, keepdims=True)
    return (x * lax.rsqrt(ms + EPS)) * g


def _pack_bf16_pair(a, b):
    lo = lax.bitcast_convert_type(a.astype(BF16).astype(F32), jnp.uint32) >> 16
    hi = lax.bitcast_convert_type(b.astype(BF16).astype(F32), jnp.uint32) & jnp.uint32(0xFFFF0000)
    return lo | hi


def _unpack_bf16_pair(word, index):
    bits = (word << 16) if index == 0 else (word & jnp.uint32(0xFFFF0000))
    return lax.bitcast_convert_type(bits, F32)


def _in_proj_kernel(x_ref, g_ref, w_ref, cos_ref, sin_ref,
                    z1_ref, z4_ref, z16_ref, gates_ref, h_ref, slab_ref,
                    *, n_rope_tiles, n_qkv_tiles):
    j = pl.program_id(1)
    tm = x_ref.shape[0]
    n_slabs = slab_ref.shape[0]

    @pl.when(j == 0)
    def _():
        h_ref[...] = _rms(x_ref[...], g_ref[...]).astype(BF16)

    def project():
        return jnp.dot(h_ref[...], w_ref[...], preferred_element_type=F32)

    def emit_qkv(rope):
        acc = project()
        if rope:
            cos = cos_ref[...]
            sin = sin_ref[...]
            lane = lax.broadcasted_iota(jnp.int32, cos.shape, 1)
            first_half = lane < ROT_DIM // 2
        for c in range(n_slabs):
            cols = slice(c * V7X_LANES, (c + 1) * V7X_LANES)
            t = acc[:, cols]
            if rope:
                hi = pltpu.roll(t, HEAD_DIM - ROT_DIM // 2, axis=1)
                lo = pltpu.roll(t, ROT_DIM // 2, axis=1)
                t = t * cos + jnp.where(first_half, hi, lo) * sin
            z1_ref[:, cols] = t.astype(BF16)
            slab_ref[c] = t
        for dil, ref in ((4, z4_ref), (16, z16_ref)):
            for r in range(dil):
                for c in range(n_slabs):
                    cols = slice(c * V7X_LANES, (c + 1) * V7X_LANES)
                    ref[r, :, cols] = slab_ref[c, pl.ds(r, tm // dil, stride=dil), :].astype(BF16)

    @pl.when(j < n_rope_tiles)
    def _():
        emit_qkv(True)

    @pl.when((j >= n_rope_tiles) & (j < n_qkv_tiles))
    def _():
        emit_qkv(False)

    @pl.when(j >= n_qkv_tiles)
    def _():
        gates_ref[...] = project().astype(BF16)


def _in_proj(x, g, w, rope, *, batch, seq_len, tm, tn):
    t, d = x.shape
    n = w.shape[1]
    assert n == QKV_W + GATES_W and t % tm == 0 and seq_len % tm == 0
    assert QKV_W % tn == 0 and GATES_W % tn == 0 and (2 * ATT_W) % tn == 0
    assert tn % V7X_LANES == 0 and tm % (16 * V7X_BF16_SUBLANES) == 0
    n_qkv_tiles = QKV_W // tn
    n_rope_tiles = 2 * ATT_W // tn
    tiles_per_seq = seq_len // tm
    last_qkv = n_qkv_tiles - 1

    def qkv_col(j):
        return jnp.minimum(j, last_qkv)

    def sorted_spec(dil):
        return pl.BlockSpec((None, dil, tm // dil, tn),
                            lambda i, j: (i // tiles_per_seq, 0, i % tiles_per_seq, qkv_col(j)))

    rope_spec = pl.BlockSpec((tm, HEAD_DIM), lambda i, j: (i % tiles_per_seq, 0))
    z1, z4, z16, gates = pl.pallas_call(
        functools.partial(_in_proj_kernel, n_rope_tiles=n_rope_tiles, n_qkv_tiles=n_qkv_tiles),
        grid=(t // tm, n // tn),
        in_specs=[pl.BlockSpec((tm, d), lambda i, j: (i, 0)),
                  pl.BlockSpec((1, d), lambda i, j: (0, 0)),
                  pl.BlockSpec((d, tn), lambda i, j: (0, j)),
                  rope_spec, rope_spec],
        out_specs=[pl.BlockSpec((tm, tn), lambda i, j: (i, qkv_col(j))),
                   sorted_spec(4), sorted_spec(16),
                   pl.BlockSpec((tm, tn), lambda i, j: (i, jnp.maximum(j - n_qkv_tiles, 0)))],
        out_shape=[jax.ShapeDtypeStruct((t, QKV_W), BF16),
                   jax.ShapeDtypeStruct((batch, 4, seq_len // 4, QKV_W), BF16),
                   jax.ShapeDtypeStruct((batch, 16, seq_len // 16, QKV_W), BF16),
                   jax.ShapeDtypeStruct((t, GATES_W), BF16)],
        scratch_shapes=[pltpu.VMEM((tm, d), BF16),
                        pltpu.VMEM((tn // V7X_LANES, tm, V7X_LANES), F32)],
        compiler_params=_params(("parallel", "arbitrary"), 40),
        name="in_proj",
    )(x, g, w, *rope)
    return z1.reshape(batch, 1, seq_len, QKV_W), z4, z16, gates


def _rope_tables(seq_len):
    inv = jnp.float32(ROPE_THETA) ** (-jnp.arange(0, ROT_DIM, 2, dtype=F32) / ROT_DIM)
    ang = jnp.arange(seq_len, dtype=F32)[:, None] * inv[None, :]
    cos, sin = jnp.cos(ang), jnp.sin(ang)
    pad = HEAD_DIM - ROT_DIM
    cos_t = jnp.concatenate([cos, cos, jnp.ones((seq_len, pad), F32)], axis=1)
    sin_t = jnp.concatenate([-sin, sin, jnp.zeros((seq_len, pad), F32)], axis=1)
    return cos_t, sin_t


def _norm_matmul_kernel(x_ref, g_ref, w_ref, o_ref, h_ref):
    @pl.when(pl.program_id(1) == 0)
    def _():
        h_ref[...] = _rms(x_ref[...], g_ref[...]).astype(BF16)

    o_ref[...] = jnp.dot(h_ref[...], w_ref[...], preferred_element_type=F32).astype(BF16)


def _norm_matmul(x, g, w, *, tm, tn):
    rows, d = x.shape
    n = w.shape[1]
    assert rows % tm == 0 and n % tn == 0
    return pl.pallas_call(
        _norm_matmul_kernel,
        grid=(rows // tm, n // tn),
        in_specs=[pl.BlockSpec((tm, d), lambda i, j: (i, 0)),
                  pl.BlockSpec((1, d), lambda i, j: (0, 0)),
                  pl.BlockSpec((d, tn), lambda i, j: (0, j))],
        out_specs=pl.BlockSpec((tm, tn), lambda i, j: (i, j)),
        out_shape=jax.ShapeDtypeStruct((rows, n), BF16),
        scratch_shapes=[pltpu.VMEM((tm, d), BF16)],
        compiler_params=_params(("parallel", "arbitrary"), 32),
        name="norm_matmul",
    )(x, g, w)


def _branch_kernel(q_ref, kp_ref, k_ref, kn_ref, vp_ref, v_ref, vn_ref, o_ref, lse_ref,
                   *, res_len, dil):
    tq = q_ref.shape[0]
    r = pl.program_id(2)
    base = pl.program_id(1) * tq
    keys = jnp.concatenate([kp_ref[...], k_ref[...], kn_ref[...]], axis=0)
    vals = jnp.concatenate([vp_ref[...], v_ref[...], vn_ref[...]], axis=0)
    tb = ATT_Q_BLOCK
    nk = tb + 2 * BAND_HALF
    qi = lax.broadcasted_iota(jnp.int32, (tb, nk), 0)
    kj = lax.broadcasted_iota(jnp.int32, (tb, nk), 1)
    off = kj - qi
    band = (off >= 0) & (off <= 2 * BAND_HALF)
    lane = lax.broadcasted_iota(jnp.int32, (tb, V7X_LANES), 1)
    scale = HEAD_DIM ** -0.5
    for blk in range(tq // tb):
        kpos = base + blk * tb - BAND_HALF + kj
        valid = band & (kpos >= 0) & (kpos < res_len)
        qrows = slice(blk * tb, (blk + 1) * tb)
        krows = slice(blk * tb, blk * tb + nk)
        if dil == 1:
            out_rows = pl.ds(blk * tb, tb)
        else:
            out_rows = pl.ds(r + dil * blk * tb, tb, stride=dil)
        lse_all = jnp.zeros((tb, V7X_LANES), F32)
        for p in range(HEAD_PAIRS):
            pair = []
            for h in (2 * p, 2 * p + 1):
                cols = slice(h * HEAD_DIM, (h + 1) * HEAD_DIM)
                s = lax.dot_general(q_ref[qrows, cols], keys[krows, cols],
                                    (((1,), (1,)), ((), ())),
                                    preferred_element_type=F32) * scale
                s = jnp.where(valid, s, NEG)
                m = jnp.max(s, axis=-1, keepdims=True)
                e = jnp.exp(s - m)
                l = jnp.sum(e, axis=-1, keepdims=True)
                o = jnp.dot(e.astype(BF16), vals[krows, cols], preferred_element_type=F32)
                pair.append(o / l)
                lse_all = jnp.where(lane == h, m + jnp.log(l), lse_all)
            o_ref[p, out_rows, :] = _pack_bf16_pair(*pair)
        lse_ref[out_rows, :] = lse_all


def _dilated_branch(z, dil):
    batch, _, res_len, _ = z.shape
    seq_len = res_len * dil
    tq = min(ATT_MAX_Q_TILE, res_len)
    assert res_len % tq == 0 and tq % ATT_Q_BLOCK == 0 and tq % BAND_HALF == 0
    halo_per_tile = tq // BAND_HALF
    n_halo_blocks = res_len // BAND_HALF

    def main(part):
        return pl.BlockSpec((None, None, tq, ATT_W), lambda b, c, r: (b, r, c, part))

    def prev(part):
        return pl.BlockSpec((None, None, BAND_HALF, ATT_W),
                            lambda b, c, r: (b, r, jnp.maximum(c * halo_per_tile - 1, 0), part))

    def nxt(part):
        return pl.BlockSpec((None, None, BAND_HALF, ATT_W),
                            lambda b, c, r: (b, r, jnp.minimum((c + 1) * halo_per_tile,
                                                               n_halo_blocks - 1), part))

    return pl.pallas_call(
        functools.partial(_branch_kernel, res_len=res_len, dil=dil),
        grid=(batch, res_len // tq, dil),
        in_specs=[main(0), prev(1), main(1), nxt(1), prev(2), main(2), nxt(2)],
        out_specs=[pl.BlockSpec((None, HEAD_PAIRS, tq * dil, V7X_LANES),
                                lambda b, c, r: (b, 0, c, 0)),
                   pl.BlockSpec((None, tq * dil, V7X_LANES), lambda b, c, r: (b, c, 0))],
        out_shape=[jax.ShapeDtypeStruct((batch, HEAD_PAIRS, seq_len, V7X_LANES), jnp.uint32),
                   jax.ShapeDtypeStruct((batch, seq_len, V7X_LANES), F32)],
        compiler_params=_params(("parallel", "parallel", "arbitrary"), 48),
        name=f"dilated_branch_d{dil}",
    )(z, z, z, z, z, z, z)


def _mixer_out_kernel(o1_ref, o2_ref, o3_ref, l1_ref, l2_ref, l3_ref,
                      gb_ref, gc_ref, gh_ref, cp_ref, hp_ref, cn_ref, hn_ref,
                      cw_ref, ga_ref, gcv_ref, wo_ref, gpost_ref, x_ref, out_ref,
                      att_ref, *, tiles_per_seq):
    i = pl.program_id(0)
    tm = x_ref.shape[0]
    l1, l2, l3 = l1_ref[...], l2_ref[...], l3_ref[...]
    mx = jnp.maximum(jnp.maximum(l1, l2), l3)
    e1, e2, e3 = jnp.exp(l1 - mx), jnp.exp(l2 - mx), jnp.exp(l3 - mx)
    den = e1 + e2 + e3
    w1, w2, w3 = e1 / den, e2 / den, e3 / den
    ssq = jnp.zeros((tm, 1), F32)
    for p in range(HEAD_PAIRS):
        for idx in range(2):
            h = 2 * p + idx
            o1, o2, o3 = (_unpack_bf16_pair(ref[p], idx) for ref in (o1_ref, o2_ref, o3_ref))
            a = w1[:, h:h + 1] * o1 + w2[:, h:h + 1] * o2 + w3[:, h:h + 1] * o3
            att_ref[:, h * HEAD_DIM:(h + 1) * HEAD_DIM] = a
            ssq = ssq + jnp.sum(a * a, axis=-1, keepdims=True)
    att_n = (att_ref[...] * lax.rsqrt(ssq / ATT_W + EPS)) * ga_ref[...]

    u = gc_ref[...].astype(F32) * gh_ref[...].astype(F32)
    last = CONV_HALO_ROWS - 1
    u_prev = cp_ref[last:, :].astype(F32) * hp_ref[last:, :].astype(F32)
    u_next = cn_ref[:1, :].astype(F32) * hn_ref[:1, :].astype(F32)
    pos = i % tiles_per_seq
    u_prev = jnp.where(pos == 0, jnp.zeros_like(u_prev), u_prev)
    u_next = jnp.where(pos == tiles_per_seq - 1, jnp.zeros_like(u_next), u_next)
    row = lax.broadcasted_iota(jnp.int32, u.shape, 0)
    up = jnp.where(row == 0, u_prev, pltpu.roll(u, 1, axis=0))
    dn = jnp.where(row == tm - 1, u_next, pltpu.roll(u, tm - 1, axis=0))
    cw = cw_ref[...]
    y = up * cw[0:1, :] + u * cw[1:2, :] + dn * cw[2:3, :]
    cnv = gb_ref[...].astype(F32) * y
    cnv_n = _rms(cnv, gcv_ref[...])

    mixed = jnp.concatenate([att_n.astype(BF16), cnv_n.astype(BF16)], axis=-1)
    mix = jnp.dot(mixed, wo_ref[...], preferred_element_type=F32)
    out_ref[...] = x_ref[...] + _rms(mix, gpost_ref[...])


def _mixer_out(x, gates, branches, conv_w, g_attn, g_conv, w_o, g_post, *, seq_len, tm):
    t = x.shape[0]
    assert t % tm == 0 and seq_len % tm == 0 and tm % CONV_HALO_ROWS == 0
    tiles_per_seq = seq_len // tm
    halo_per_tile = tm // CONV_HALO_ROWS
    n_halo = t // CONV_HALO_ROWS

    def rows(width):
        return pl.BlockSpec((tm, width), lambda i: (i, 0))

    packed = pl.BlockSpec((None, HEAD_PAIRS, tm, V7X_LANES),
                          lambda i: (i // tiles_per_seq, 0, i % tiles_per_seq, 0))
    lse = pl.BlockSpec((None, tm, V7X_LANES),
                       lambda i: (i // tiles_per_seq, i % tiles_per_seq, 0))

    def gate(k):
        return pl.BlockSpec((tm, CONV_W), lambda i: (i, k))

    def halo_prev(k):
        return pl.BlockSpec((CONV_HALO_ROWS, CONV_W),
                            lambda i: (jnp.maximum(i * halo_per_tile - 1, 0), k))

    def halo_next(k):
        return pl.BlockSpec((CONV_HALO_ROWS, CONV_W),
                            lambda i: (jnp.minimum((i + 1) * halo_per_tile, n_halo - 1), k))

    def vec(width, nrows=1):
        return pl.BlockSpec((nrows, width), lambda i: (0, 0))

    (o1, l1), (o2, l2), (o3, l3) = branches
    return pl.pallas_call(
        functools.partial(_mixer_out_kernel, tiles_per_seq=tiles_per_seq),
        grid=(t // tm,),
        in_specs=[packed, packed, packed, lse, lse, lse,
                  gate(0), gate(1), gate(2),
                  halo_prev(1), halo_prev(2), halo_next(1), halo_next(2),
                  vec(CONV_W, 3), vec(ATT_W), vec(CONV_W),
                  _resident((D_MODEL, D_MODEL), lambda i: (0, 0)),
                  vec(D_MODEL), rows(D_MODEL)],
        out_specs=rows(D_MODEL),
        out_shape=jax.ShapeDtypeStruct((t, D_MODEL), F32),
        scratch_shapes=[pltpu.VMEM((tm, ATT_W), F32)],
        compiler_params=_params(("parallel",), 48),
        name="mixer_out",
    )(o1, o2, o3, l1, l2, l3, gates, gates, gates, gates, gates, gates, gates,
      conv_w, g_attn, g_conv, w_o, g_post, x)


def _cross_attn_kernel(x_ref, gpre_ref, wq_ref, k_ref, v_ref, wo_ref, gpost_ref, out_ref):
    x = x_ref[...]
    xn = _rms(x, gpre_ref[...]).astype(BF16)
    q = jnp.dot(xn, wq_ref[...], preferred_element_type=F32).astype(BF16)
    scale = X_HEAD_DIM ** -0.5
    heads = []
    for h in range(X_HEADS):
        cols = slice(h * X_HEAD_DIM, (h + 1) * X_HEAD_DIM)
        s = lax.dot_general(q[:, cols], k_ref[:, cols], (((1,), (1,)), ((), ())),
                            preferred_element_type=F32) * scale
        m = jnp.max(s, axis=-1, keepdims=True)
        e = jnp.exp(s - m)
        p = e / jnp.sum(e, axis=-1, keepdims=True)
        o = jnp.dot(p.astype(BF16), v_ref[:, cols], preferred_element_type=F32)
        heads.append(o.astype(BF16))
    o = jnp.concatenate(heads, axis=-1)
    xa = jnp.dot(o, wo_ref[...], preferred_element_type=F32)
    out_ref[...] = x + _rms(xa, gpost_ref[...])


def _cross_attn(x, kv, g_pre, w_q, w_o, g_post, *, seq_len, tm):
    t = x.shape[0]
    assert t % tm == 0 and seq_len % tm == 0
    tiles_per_seq = seq_len // tm
    vec = pl.BlockSpec((1, D_MODEL), lambda i: (0, 0))
    rows = pl.BlockSpec((tm, D_MODEL), lambda i: (i, 0))
    return pl.pallas_call(
        _cross_attn_kernel,
        grid=(t // tm,),
        in_specs=[rows, vec,
                  _resident((D_MODEL, D_MODEL), lambda i: (0, 0)),
                  pl.BlockSpec((N_MEM, D_MODEL), lambda i: (i // tiles_per_seq, 0)),
                  pl.BlockSpec((N_MEM, D_MODEL), lambda i: (i // tiles_per_seq, 1)),
                  _resident((D_MODEL, D_MODEL), lambda i: (0, 0)),
                  vec],
        out_specs=rows,
        out_shape=jax.ShapeDtypeStruct((t, D_MODEL), F32),
        compiler_params=_params(("parallel",), 56),
        name="cross_attn",
    )(x, g_pre, w_q, kv, kv, w_o, g_post)


def _ffn_kernel(x_ref, gpre_ref, wg_ref, wu_ref, wd_ref, gpost_ref, out_ref, h_ref):
    j = pl.program_id(1)

    @pl.when(j == 0)
    def _():
        h_ref[...] = _rms(x_ref[...], gpre_ref[...]).astype(BF16)

    h = h_ref[...]
    gate = jnp.dot(h, wg_ref[...], preferred_element_type=F32)
    up = jnp.dot(h, wu_ref[...], preferred_element_type=F32)
    act = (gate * jax.nn.sigmoid(gate) * up).astype(BF16)
    part = jnp.dot(act, wd_ref[...], preferred_element_type=F32)

    @pl.when(j == 0)
    def _():
        out_ref[...] = part

    @pl.when(j > 0)
    def _():
        out_ref[...] += part

    @pl.when(j == pl.num_programs(1) - 1)
    def _():
        out_ref[...] = x_ref[...] + _rms(out_ref[...], gpost_ref[...])


def _ffn(x, g_pre, w_gate, w_up, w_down, g_post, *, tm, tf):
    t = x.shape[0]
    assert t % tm == 0 and D_FF % tf == 0
    vec = pl.BlockSpec((1, D_MODEL), lambda i, j: (0, 0))
    rows = pl.BlockSpec((tm, D_MODEL), lambda i, j: (i, 0))
    return pl.pallas_call(
        _ffn_kernel,
        grid=(t // tm, D_FF // tf),
        in_specs=[rows, vec,
                  pl.BlockSpec((D_MODEL, tf), lambda i, j: (0, j)),
                  pl.BlockSpec((D_MODEL, tf), lambda i, j: (0, j)),
                  pl.BlockSpec((tf, D_MODEL), lambda i, j: (j, 0)),
                  vec],
        out_specs=rows,
        out_shape=jax.ShapeDtypeStruct((t, D_MODEL), F32),
        scratch_shapes=[pltpu.VMEM((tm, D_MODEL), BF16)],
        compiler_params=_params(("parallel", "arbitrary"), 48),
        name="swiglu_ffn",
    )(x, g_pre, w_gate, w_up, w_down, g_post)


def _trunk(x, mem, layers, rope):
    batch, seq_len, _ = x.shape
    x = x.reshape(batch * seq_len, D_MODEL)
    mem = mem.reshape(batch * N_MEM, D_MODEL)
    for p in layers:
        *zs, gates = _in_proj(x, p["g_mix_pre"], p["w_in"], rope, batch=batch,
                              seq_len=seq_len, tm=ROW_TILE, tn=COL_TILE)
        branches = [_dilated_branch(z, dil) for z, dil in zip(zs, DILATIONS)]
        x = _mixer_out(x, gates, branches, p["conv_w"], p["g_attn_out"], p["g_conv_out"],
                       p["w_o"], p["g_mix_post"], seq_len=seq_len, tm=ROW_TILE)
        kv = _norm_matmul(mem, p["g_mem"], p["w_xkv"], tm=N_MEM, tn=COL_TILE)
        x = _cross_attn(x, kv, p["g_x_pre"], p["w_xq"], p["w_xo"], p["g_x_post"],
                        seq_len=seq_len, tm=ROW_TILE)
        x = _ffn(x, p["g_ffn_pre"], p["w_gate"], p["w_up"], p["w_down"], p["g_ffn_post"],
                 tm=ROW_TILE, tf=FFN_COL_TILE)
    return x.reshape(batch, seq_len, D_MODEL)


def kernel(x_prompt, x_sample, mem_prompt, mem_sample, g_mix_pre, w_in, conv_w, g_attn_out, g_conv_out, w_o, g_mix_post, g_x_pre, g_mem, w_xq, w_xk, w_xv, w_xo, g_x_post, g_ffn_pre, w_gate, w_up, w_down, g_ffn_post):
    depth = w_in.shape[0]
    layers = []
    for l in range(depth):
        layers.append({
            "g_mix_pre": g_mix_pre[l][None, :], "w_in": w_in[l].astype(BF16),
            "conv_w": conv_w[l], "g_attn_out": g_attn_out[l][None, :],
            "g_conv_out": g_conv_out[l][None, :], "w_o": w_o[l].astype(BF16),
            "g_mix_post": g_mix_post[l][None, :], "g_x_pre": g_x_pre[l][None, :],
            "g_mem": g_mem[l][None, :], "w_xq": w_xq[l].astype(BF16),
            "w_xkv": jnp.concatenate([w_xk[l], w_xv[l]], axis=1).astype(BF16),
            "w_xo": w_xo[l].astype(BF16), "g_x_post": g_x_post[l][None, :],
            "g_ffn_pre": g_ffn_pre[l][None, :], "w_gate": w_gate[l].astype(BF16),
            "w_up": w_up[l].astype(BF16), "w_down": w_down[l].astype(BF16),
            "g_ffn_post": g_ffn_post[l][None, :],
        })
    outs = []
    for x, mem in ((x_prompt, mem_prompt), (x_sample, mem_sample)):
        rope = _rope_tables(x.shape[1])
        outs.append(_trunk(x, mem, layers, rope))
    return tuple(outs)
```

```python
import functools

import jax
import jax.numpy as jnp
from jax import lax
from jax.experimental import pallas as pl
from jax.experimental.pallas import tpu as pltpu

F32 = jnp.float32
BF16 = jnp.bfloat16

D_MODEL = 2048
HEAD_DIM = 128
N_ATT_HEADS = 12
ATT_W = N_ATT_HEADS * HEAD_DIM
QKV_W = 3 * ATT_W
CONV_W = D_MODEL - ATT_W
GATES_W = 3 * CONV_W
DILATIONS = (1, 4, 16)
BAND_HALF = 64
ROT_DIM = HEAD_DIM // 4
ROPE_THETA = 500000.0
N_MEM = 256
X_HEADS = 4
X_HEAD_DIM = D_MODEL // X_HEADS
D_FF = 5632
EPS = 1e-6
NEG = -1e30

V7X_LANES = 128
V7X_BF16_SUBLANES = 16
V7X_VMEM_BYTES = 64 * 1024 * 1024
MIB = 1024 * 1024

ROW_TILE = 512
IN_PROJ_ROW_TILE = 1024
IN_PROJ_ROW_CHUNK = 256
COL_TILE = 512
FFN_COL_TILE = 512
FFN_ROW_CHUNK = 256
ATT_Q_BLOCK = 128
ATT_MAX_Q_TILE = 256
CONV_HALO_ROWS = V7X_BF16_SUBLANES
HEAD_PAIRS = N_ATT_HEADS // 2


def _params(semantics, vmem_mib):
    assert vmem_mib * MIB < V7X_VMEM_BYTES
    return pltpu.CompilerParams(dimension_semantics=semantics,
                                vmem_limit_bytes=vmem_mib * MIB)


def _resident(shape, index_map):
    return pl.BlockSpec(shape, index_map, pipeline_mode=pl.Buffered(1))


def _rms(x, g):
    ms = jnp.mean(x * x, axis=-1, keepdims=True)
    return (x * lax.rsqrt(ms + EPS)) * g


def _pack_bf16_pair(a, b):
    lo = lax.bitcast_convert_type(a.astype(BF16).astype(F32), jnp.uint32) >> 16
    hi = lax.bitcast_convert_type(b.astype(BF16).astype(F32), jnp.uint32) & jnp.uint32(0xFFFF0000)
    return lo | hi


def _unpack_bf16_pair(word, index):
    bits = (word << 16) if index == 0 else (word & jnp.uint32(0xFFFF0000))
    return lax.bitcast_convert_type(bits, F32)


def _in_proj_kernel(x_ref, g_ref, w_ref, cos_ref, sin_ref,
                    z1_ref, z4_ref, z16_ref, gates_ref, h_ref, slab_ref,
                    *, n_rope_tiles, n_qkv_tiles):
    j = pl.program_id(1)
    tm = x_ref.shape[0]
    n_slabs = slab_ref.shape[0]

    @pl.when(j == 0)
    def _():
        h_ref[...] = _rms(x_ref[...], g_ref[...]).astype(BF16)

    def project(rows):
        return jnp.dot(h_ref[rows, :], w_ref[...], preferred_element_type=F32)

    rc = IN_PROJ_ROW_CHUNK

    def emit_qkv(rope):
        for ch in range(tm // rc):
            rows = slice(ch * rc, (ch + 1) * rc)
            acc = project(rows)
            if rope:
                cos = cos_ref[rows, :]
                sin = sin_ref[rows, :]
                lane = lax.broadcasted_iota(jnp.int32, cos.shape, 1)
                first_half = lane < ROT_DIM // 2
            for c in range(n_slabs):
                cols = slice(c * V7X_LANES, (c + 1) * V7X_LANES)
                t = acc[:, cols]
                if rope:
                    hi = pltpu.roll(t, HEAD_DIM - ROT_DIM // 2, axis=1)
                    lo = pltpu.roll(t, ROT_DIM // 2, axis=1)
                    t = t * cos + jnp.where(first_half, hi, lo) * sin
                z1_ref[rows, cols] = t.astype(BF16)
                slab_ref[c, rows, :] = t
            for dil, ref in ((4, z4_ref), (16, z16_ref)):
                n = rc // dil
                for r in range(dil):
                    for c in range(n_slabs):
                        cols = slice(c * V7X_LANES, (c + 1) * V7X_LANES)
                        src = slab_ref[c, pl.ds(ch * rc + r, n, stride=dil), :]
                        ref[r, ch * n:(ch + 1) * n, cols] = src.astype(BF16)

    @pl.when(j < n_rope_tiles)
    def _():
        emit_qkv(True)

    @pl.when((j >= n_rope_tiles) & (j < n_qkv_tiles))
    def _():
        emit_qkv(False)

    @pl.when(j >= n_qkv_tiles)
    def _():
        for ch in range(tm // rc):
            rows = slice(ch * rc, (ch + 1) * rc)
            gates_ref[rows, :] = project(rows).astype(BF16)


def _in_proj(x, g, w, rope, *, batch, seq_len, tm, tn):
    t, d = x.shape
    n = w.shape[1]
    assert n == QKV_W + GATES_W and t % tm == 0 and seq_len % tm == 0
    assert QKV_W % tn == 0 and GATES_W % tn == 0 and (2 * ATT_W) % tn == 0
    assert tn % V7X_LANES == 0 and tm % (16 * V7X_BF16_SUBLANES) == 0
    n_qkv_tiles = QKV_W // tn
    n_rope_tiles = 2 * ATT_W // tn
    tiles_per_seq = seq_len // tm
    last_qkv = n_qkv_tiles - 1

    def qkv_col(j):
        return jnp.minimum(j, last_qkv)

    def sorted_spec(dil):
        return pl.BlockSpec((None, dil, tm // dil, tn),
                            lambda i, j: (i // tiles_per_seq, 0, i % tiles_per_seq, qkv_col(j)))

    rope_spec = pl.BlockSpec((tm, HEAD_DIM), lambda i, j: (i % tiles_per_seq, 0))
    z1, z4, z16, gates = pl.pallas_call(
        functools.partial(_in_proj_kernel, n_rope_tiles=n_rope_tiles, n_qkv_tiles=n_qkv_tiles),
        grid=(t // tm, n // tn),
        in_specs=[pl.BlockSpec((tm, d), lambda i, j: (i, 0)),
                  pl.BlockSpec((1, d), lambda i, j: (0, 0)),
                  pl.BlockSpec((d, tn), lambda i, j: (0, j)),
                  rope_spec, rope_spec],
        out_specs=[pl.BlockSpec((tm, tn), lambda i, j: (i, qkv_col(j))),
                   sorted_spec(4), sorted_spec(16),
                   pl.BlockSpec((tm, tn), lambda i, j: (i, jnp.maximum(j - n_qkv_tiles, 0)))],
        out_shape=[jax.ShapeDtypeStruct((t, QKV_W), BF16),
                   jax.ShapeDtypeStruct((batch, 4, seq_len // 4, QKV_W), BF16),
                   jax.ShapeDtypeStruct((batch, 16, seq_len // 16, QKV_W), BF16),
                   jax.ShapeDtypeStruct((t, GATES_W), BF16)],
        scratch_shapes=[pltpu.VMEM((tm, d), BF16),
                        pltpu.VMEM((tn // V7X_LANES, tm, V7X_LANES), F32)],
        compiler_params=_params(("parallel", "arbitrary"), 48),
        name="in_proj",
    )(x, g, w, *rope)
    return z1.reshape(batch, 1, seq_len, QKV_W), z4, z16, gates


def _rope_tables(seq_len):
    inv = jnp.float32(ROPE_THETA) ** (-jnp.arange(0, ROT_DIM, 2, dtype=F32) / ROT_DIM)
    ang = jnp.arange(seq_len, dtype=F32)[:, None] * inv[None, :]
    cos, sin = jnp.cos(ang), jnp.sin(ang)
    pad = HEAD_DIM - ROT_DIM
    cos_t = jnp.concatenate([cos, cos, jnp.ones((seq_len, pad), F32)], axis=1)
    sin_t = jnp.concatenate([-sin, sin, jnp.zeros((seq_len, pad), F32)], axis=1)
    return cos_t, sin_t


def _norm_matmul_kernel(x_ref, g_ref, w_ref, o_ref, h_ref):
    @pl.when(pl.program_id(1) == 0)
    def _():
        h_ref[...] = _rms(x_ref[...], g_ref[...]).astype(BF16)

    o_ref[...] = jnp.dot(h_ref[...], w_ref[...], preferred_element_type=F32).astype(BF16)


def _norm_matmul(x, g, w, *, tm, tn):
    rows, d = x.shape
    n = w.shape[1]
    assert rows % tm == 0 and n % tn == 0
    return pl.pallas_call(
        _norm_matmul_kernel,
        grid=(rows // tm, n // tn),
        in_specs=[pl.BlockSpec((tm, d), lambda i, j: (i, 0)),
                  pl.BlockSpec((1, d), lambda i, j: (0, 0)),
                  pl.BlockSpec((d, tn), lambda i, j: (0, j))],
        out_specs=pl.BlockSpec((tm, tn), lambda i, j: (i, j)),
        out_shape=jax.ShapeDtypeStruct((rows, n), BF16),
        scratch_shapes=[pltpu.VMEM((tm, d), BF16)],
        compiler_params=_params(("parallel", "arbitrary"), 32),
        name="norm_matmul",
    )(x, g, w)


def _branch_kernel(q_ref, kp_ref, k_ref, kn_ref, vp_ref, v_ref, vn_ref, o_ref, lse_ref,
                   *, res_len, dil):
    tq = q_ref.shape[0]
    r = pl.program_id(2)
    base = pl.program_id(1) * tq
    keys = jnp.concatenate([kp_ref[...], k_ref[...], kn_ref[...]], axis=0)
    vals = jnp.concatenate([vp_ref[...], v_ref[...], vn_ref[...]], axis=0)
    tb = ATT_Q_BLOCK
    nk = tb + 2 * BAND_HALF
    qi = lax.broadcasted_iota(jnp.int32, (tb, nk), 0)
    kj = lax.broadcasted_iota(jnp.int32, (tb, nk), 1)
    off = kj - qi
    band = (off >= 0) & (off <= 2 * BAND_HALF)
    lane = lax.broadcasted_iota(jnp.int32, (tb, V7X_LANES), 1)
    scale = HEAD_DIM ** -0.5
    for blk in range(tq // tb):
        kpos = base + blk * tb - BAND_HALF + kj
        valid = band & (kpos >= 0) & (kpos < res_len)
        qrows = slice(blk * tb, (blk + 1) * tb)
        krows = slice(blk * tb, blk * tb + nk)
        if dil == 1:
            out_rows = pl.ds(blk * tb, tb)
        else:
            out_rows = pl.ds(r + dil * blk * tb, tb, stride=dil)
        lse_all = jnp.zeros((tb, V7X_LANES), F32)
        for p in range(HEAD_PAIRS):
            pair = []
            for h in (2 * p, 2 * p + 1):
                cols = slice(h * HEAD_DIM, (h + 1) * HEAD_DIM)
                s = lax.dot_general(q_ref[qrows, cols], keys[krows, cols],
                                    (((1,), (1,)), ((), ())),
                                    preferred_element_type=F32) * scale
                s = jnp.where(valid, s, NEG)
                m = jnp.max(s, axis=-1, keepdims=True)
                e = jnp.exp(s - m)
                l = jnp.sum(e, axis=-1, keepdims=True)
                o = jnp.dot(e.astype(BF16), vals[krows, cols], preferred_element_type=F32)
                pair.append(o / l)
                lse_all = jnp.where(lane == h, m + jnp.log(l), lse_all)
            o_ref[p, out_rows, :] = _pack_bf16_pair(*pair)
        lse_ref[out_rows, :] = lse_all


def _dilated_branch(z, dil):
    batch, _, res_len, _ = z.shape
    seq_len = res_len * dil
    tq = min(ATT_MAX_Q_TILE, res_len)
    assert res_len % tq == 0 and tq % ATT_Q_BLOCK == 0 and tq % BAND_HALF == 0
    halo_per_tile = tq // BAND_HALF
    n_halo_blocks = res_len // BAND_HALF

    def main(part):
        return pl.BlockSpec((None, None, tq, ATT_W), lambda b, c, r: (b, r, c, part))

    def prev(part):
        return pl.BlockSpec((None, None, BAND_HALF, ATT_W),
                            lambda b, c, r: (b, r, jnp.maximum(c * halo_per_tile - 1, 0), part))

    def nxt(part):
        return pl.BlockSpec((None, None, BAND_HALF, ATT_W),
                            lambda b, c, r: (b, r, jnp.minimum((c + 1) * halo_per_tile,
                                                               n_halo_blocks - 1), part))

    return pl.pallas_call(
        functools.partial(_branch_kernel, res_len=res_len, dil=dil),
        grid=(batch, res_len // tq, dil),
        in_specs=[main(0), prev(1), main(1), nxt(1), prev(2), main(2), nxt(2)],
        out_specs=[pl.BlockSpec((None, HEAD_PAIRS, tq * dil, V7X_LANES),
                                lambda b, c, r: (b, 0, c, 0)),
                   pl.BlockSpec((None, tq * dil, V7X_LANES), lambda b, c, r: (b, c, 0))],
        out_shape=[jax.ShapeDtypeStruct((batch, HEAD_PAIRS, seq_len, V7X_LANES), jnp.uint32),
                   jax.ShapeDtypeStruct((batch, seq_len, V7X_LANES), F32)],
        compiler_params=_params(("parallel", "parallel", "arbitrary"), 48),
        name=f"dilated_branch_d{dil}",
    )(z, z, z, z, z, z, z)


def _mixer_out_kernel(o1_ref, o2_ref, o3_ref, l1_ref, l2_ref, l3_ref,
                      gb_ref, gc_ref, gh_ref, cp_ref, hp_ref, cn_ref, hn_ref,
                      cw_ref, ga_ref, gcv_ref, wo_ref, gpost_ref, x_ref, out_ref,
                      att_ref, *, tiles_per_seq):
    i = pl.program_id(0)
    tm = x_ref.shape[0]
    l1, l2, l3 = l1_ref[...], l2_ref[...], l3_ref[...]
    mx = jnp.maximum(jnp.maximum(l1, l2), l3)
    e1, e2, e3 = jnp.exp(l1 - mx), jnp.exp(l2 - mx), jnp.exp(l3 - mx)
    den = e1 + e2 + e3
    w1, w2, w3 = e1 / den, e2 / den, e3 / den
    ssq = jnp.zeros((tm, 1), F32)
    for p in range(HEAD_PAIRS):
        for idx in range(2):
            h = 2 * p + idx
            o1, o2, o3 = (_unpack_bf16_pair(ref[p], idx) for ref in (o1_ref, o2_ref, o3_ref))
            a = w1[:, h:h + 1] * o1 + w2[:, h:h + 1] * o2 + w3[:, h:h + 1] * o3
            att_ref[:, h * HEAD_DIM:(h + 1) * HEAD_DIM] = a
            ssq = ssq + jnp.sum(a * a, axis=-1, keepdims=True)
    att_n = (att_ref[...] * lax.rsqrt(ssq / ATT_W + EPS)) * ga_ref[...]

    u = gc_ref[...].astype(F32) * gh_ref[...].astype(F32)
    last = CONV_HALO_ROWS - 1
    u_prev = cp_ref[last:, :].astype(F32) * hp_ref[last:, :].astype(F32)
    u_next = cn_ref[:1, :].astype(F32) * hn_ref[:1, :].astype(F32)
    pos = i % tiles_per_seq
    u_prev = jnp.where(pos == 0, jnp.zeros_like(u_prev), u_prev)
    u_next = jnp.where(pos == tiles_per_seq - 1, jnp.zeros_like(u_next), u_next)
    row = lax.broadcasted_iota(jnp.int32, u.shape, 0)
    up = jnp.where(row == 0, u_prev, pltpu.roll(u, 1, axis=0))
    dn = jnp.where(row == tm - 1, u_next, pltpu.roll(u, tm - 1, axis=0))
    cw = cw_ref[...]
    y = up * cw[0:1, :] + u * cw[1:2, :] + dn * cw[2:3, :]
    cnv = gb_ref[...].astype(F32) * y
    cnv_n = _rms(cnv, gcv_ref[...])

    mixed = jnp.concatenate([att_n.astype(BF16), cnv_n.astype(BF16)], axis=-1)
    mix = jnp.dot(mixed, wo_ref[...], preferred_element_type=F32)
    out_ref[...] = x_ref[...] + _rms(mix, gpost_ref[...])


def _mixer_out(x, gates, branches, conv_w, g_attn, g_conv, w_o, g_post, *, seq_len, tm):
    t = x.shape[0]
    assert t % tm == 0 and seq_len % tm == 0 and tm % CONV_HALO_ROWS == 0
    tiles_per_seq = seq_len // tm
    halo_per_tile = tm // CONV_HALO_ROWS
    n_halo = t // CONV_HALO_ROWS

    def rows(width):
        return pl.BlockSpec((tm, width), lambda i: (i, 0))

    packed = pl.BlockSpec((None, HEAD_PAIRS, tm, V7X_LANES),
                          lambda i: (i // tiles_per_seq, 0, i % tiles_per_seq, 0))
    lse = pl.BlockSpec((None, tm, V7X_LANES),
                       lambda i: (i // tiles_per_seq, i % tiles_per_seq, 0))

    def gate(k):
        return pl.BlockSpec((tm, CONV_W), lambda i: (i, k))

    def halo_prev(k):
        return pl.BlockSpec((CONV_HALO_ROWS, CONV_W),
                            lambda i: (jnp.maximum(i * halo_per_tile - 1, 0), k))

    def halo_next(k):
        return pl.BlockSpec((CONV_HALO_ROWS, CONV_W),
                            lambda i: (jnp.minimum((i + 1) * halo_per_tile, n_halo - 1), k))

    def vec(width, nrows=1):
        return pl.BlockSpec((nrows, width), lambda i: (0, 0))

    (o1, l1), (o2, l2), (o3, l3) = branches
    return pl.pallas_call(
        functools.partial(_mixer_out_kernel, tiles_per_seq=tiles_per_seq),
        grid=(t // tm,),
        in_specs=[packed, packed, packed, lse, lse, lse,
                  gate(0), gate(1), gate(2),
                  halo_prev(1), halo_prev(2), halo_next(1), halo_next(2),
                  vec(CONV_W, 3), vec(ATT_W), vec(CONV_W),
                  _resident((D_MODEL, D_MODEL), lambda i: (0, 0)),
                  vec(D_MODEL), rows(D_MODEL)],
        out_specs=rows(D_MODEL),
        out_shape=jax.ShapeDtypeStruct((t, D_MODEL), F32),
        scratch_shapes=[pltpu.VMEM((tm, ATT_W), F32)],
        compiler_params=_params(("parallel",), 48),
        name="mixer_out",
    )(o1, o2, o3, l1, l2, l3, gates, gates, gates, gates, gates, gates, gates,
      conv_w, g_attn, g_conv, w_o, g_post, x)


def _cross_attn_kernel(x_ref, gpre_ref, wq_ref, k_ref, v_ref, wo_ref, gpost_ref, out_ref):
    x = x_ref[...]
    xn = _rms(x, gpre_ref[...]).astype(BF16)
    q = jnp.dot(xn, wq_ref[...], preferred_element_type=F32).astype(BF16)
    scale = X_HEAD_DIM ** -0.5
    heads = []
    for h in range(X_HEADS):
        cols = slice(h * X_HEAD_DIM, (h + 1) * X_HEAD_DIM)
        s = lax.dot_general(q[:, cols], k_ref[:, cols], (((1,), (1,)), ((), ())),
                            preferred_element_type=F32) * scale
        m = jnp.max(s, axis=-1, keepdims=True)
        e = jnp.exp(s - m)
        p = e / jnp.sum(e, axis=-1, keepdims=True)
        o = jnp.dot(p.astype(BF16), v_ref[:, cols], preferred_element_type=F32)
        heads.append(o.astype(BF16))
    o = jnp.concatenate(heads, axis=-1)
    xa = jnp.dot(o, wo_ref[...], preferred_element_type=F32)
    out_ref[...] = x + _rms(xa, gpost_ref[...])


def _cross_attn(x, kv, g_pre, w_q, w_o, g_post, *, seq_len, tm):
    t = x.shape[0]
    assert t % tm == 0 and seq_len % tm == 0
    tiles_per_seq = seq_len // tm
    vec = pl.BlockSpec((1, D_MODEL), lambda i: (0, 0))
    rows = pl.BlockSpec((tm, D_MODEL), lambda i: (i, 0))
    return pl.pallas_call(
        _cross_attn_kernel,
        grid=(t // tm,),
        in_specs=[rows, vec,
                  _resident((D_MODEL, D_MODEL), lambda i: (0, 0)),
                  pl.BlockSpec((N_MEM, D_MODEL), lambda i: (i // tiles_per_seq, 0)),
                  pl.BlockSpec((N_MEM, D_MODEL), lambda i: (i // tiles_per_seq, 1)),
                  _resident((D_MODEL, D_MODEL), lambda i: (0, 0)),
                  vec],
        out_specs=rows,
        out_shape=jax.ShapeDtypeStruct((t, D_MODEL), F32),
        compiler_params=_params(("parallel",), 56),
        name="cross_attn",
    )(x, g_pre, w_q, kv, kv, w_o, g_post)


def _ffn_kernel(x_ref, gpre_ref, wg_ref, wu_ref, wd_ref, gpost_ref, out_ref, h_ref):
    j = pl.program_id(1)

    @pl.when(j == 0)
    def _():
        h_ref[...] = _rms(x_ref[...], gpre_ref[...]).astype(BF16)
        out_ref[...] = jnp.zeros_like(out_ref)

    rc = FFN_ROW_CHUNK
    for ch in range(x_ref.shape[0] // rc):
        rows = slice(ch * rc, (ch + 1) * rc)
        h = h_ref[rows, :]
        gate = jnp.dot(h, wg_ref[...], preferred_element_type=F32)
        up = jnp.dot(h, wu_ref[...], preferred_element_type=F32)
        act = (gate * jax.nn.sigmoid(gate) * up).astype(BF16)
        out_ref[rows, :] += jnp.dot(act, wd_ref[...], preferred_element_type=F32)

    @pl.when(j == pl.num_programs(1) - 1)
    def _():
        out_ref[...] = x_ref[...] + _rms(out_ref[...], gpost_ref[...])


def _ffn(x, g_pre, w_gate, w_up, w_down, g_post, *, tm, tf):
    t = x.shape[0]
    assert t % tm == 0 and D_FF % tf == 0
    vec = pl.BlockSpec((1, D_MODEL), lambda i, j: (0, 0))
    rows = pl.BlockSpec((tm, D_MODEL), lambda i, j: (i, 0))
    return pl.pallas_call(
        _ffn_kernel,
        grid=(t // tm, D_FF // tf),
        in_specs=[rows, vec,
                  pl.BlockSpec((D_MODEL, tf), lambda i, j: (0, j)),
                  pl.BlockSpec((D_MODEL, tf), lambda i, j: (0, j)),
                  pl.BlockSpec((tf, D_MODEL), lambda i, j: (j, 0)),
                  vec],
        out_specs=rows,
        out_shape=jax.ShapeDtypeStruct((t, D_MODEL), F32),
        scratch_shapes=[pltpu.VMEM((tm, D_MODEL), BF16)],
        compiler_params=_params(("parallel", "arbitrary"), 48),
        name="swiglu_ffn",
    )(x, g_pre, w_gate, w_up, w_down, g_post)


def _trunk(x, mem, layers, rope):
    batch, seq_len, _ = x.shape
    x = x.reshape(batch * seq_len, D_MODEL)
    mem = mem.reshape(batch * N_MEM, D_MODEL)
    for p in layers:
        *zs, gates = _in_proj(x, p["g_mix_pre"], p["w_in"], rope, batch=batch,
                              seq_len=seq_len, tm=IN_PROJ_ROW_TILE, tn=COL_TILE)
        branches = [_dilated_branch(z, dil) for z, dil in zip(zs, DILATIONS)]
        x = _mixer_out(x, gates, branches, p["conv_w"], p["g_attn_out"], p["g_conv_out"],
                       p["w_o"], p["g_mix_post"], seq_len=seq_len, tm=ROW_TILE)
        kv = _norm_matmul(mem, p["g_mem"], p["w_xkv"], tm=N_MEM, tn=COL_TILE)
        x = _cross_attn(x, kv, p["g_x_pre"], p["w_xq"], p["w_xo"], p["g_x_post"],
                        seq_len=seq_len, tm=ROW_TILE)
        x = _ffn(x, p["g_ffn_pre"], p["w_gate"], p["w_up"], p["w_down"], p["g_ffn_post"],
                 tm=ROW_TILE, tf=FFN_COL_TILE)
    return x.reshape(batch, seq_len, D_MODEL)


def kernel(x_prompt, x_sample, mem_prompt, mem_sample, g_mix_pre, w_in, conv_w, g_attn_out, g_conv_out, w_o, g_mix_post, g_x_pre, g_mem, w_xq, w_xk, w_xv, w_xo, g_x_post, g_ffn_pre, w_gate, w_up, w_down, g_ffn_post):
    depth = w_in.shape[0]
    layers = []
    for l in range(depth):
        layers.append({
            "g_mix_pre": g_mix_pre[l][None, :], "w_in": w_in[l].astype(BF16),
            "conv_w": conv_w[l], "g_attn_out": g_attn_out[l][None, :],
            "g_conv_out": g_conv_out[l][None, :], "w_o": w_o[l].astype(BF16),
            "g_mix_post": g_mix_post[l][None, :], "g_x_pre": g_x_pre[l][None, :],
            "g_mem": g_mem[l][None, :], "w_xq": w_xq[l].astype(BF16),
            "w_xkv": jnp.concatenate([w_xk[l], w_xv[l]], axis=1).astype(BF16),
            "w_xo": w_xo[l].astype(BF16), "g_x_post": g_x_post[l][None, :],
            "g_ffn_pre": g_ffn_pre[l][None, :], "w_gate": w_gate[l].astype(BF16),
            "w_up": w_up[l].astype(BF16), "w_down": w_down[l].astype(BF16),
            "g_ffn_post": g_ffn_post[l][None, :],
        })
    outs = []
    for x, mem in ((x_prompt, mem_prompt), (x_sample, mem_sample)):
        rope = _rope_tables(x.shape[1])
        outs.append(_trunk(x, mem, layers, rope))
    return tuple(outs)
```

```python
import functools

import jax
import jax.numpy as jnp
from jax import lax
from jax.experimental import pallas as pl
from jax.experimental.pallas import tpu as pltpu

F32 = jnp.float32
BF16 = jnp.bfloat16

D_MODEL = 2048
HEAD_DIM = 128
N_ATT_HEADS = 12
ATT_W = N_ATT_HEADS * HEAD_DIM
QKV_W = 3 * ATT_W
CONV_W = D_MODEL - ATT_W
GATES_W = 3 * CONV_W
DILATIONS = (1, 4, 16)
BAND_HALF = 64
ROT_DIM = HEAD_DIM // 4
ROPE_THETA = 500000.0
N_MEM = 256
X_HEADS = 4
X_HEAD_DIM = D_MODEL // X_HEADS
D_FF = 5632
EPS = 1e-6
NEG = -1e30

V7X_LANES = 128
V7X_BF16_SUBLANES = 16
V7X_VMEM_BYTES = 64 * 1024 * 1024
MIB = 1024 * 1024

ROW_TILE = 512
IN_PROJ_ROW_TILE = 256
FFN_ROW_TILE = 1024
COL_TILE = 512
FFN_COL_TILE = 512
FFN_ROW_CHUNK = 256
ATT_Q_BLOCK = 128
ATT_MAX_Q_TILE = 256
CONV_HALO_ROWS = V7X_BF16_SUBLANES
HEAD_PAIRS = N_ATT_HEADS // 2


def _params(semantics, vmem_mib):
    assert vmem_mib * MIB < V7X_VMEM_BYTES
    return pltpu.CompilerParams(dimension_semantics=semantics,
                                vmem_limit_bytes=vmem_mib * MIB)


def _resident(shape, index_map):
    return pl.BlockSpec(shape, index_map, pipeline_mode=pl.Buffered(1))


def _rms(x, g):
    ms = jnp.mean(x * x, axis=-1, keepdims=True)
    return (x * lax.rsqrt(ms + EPS)) * g


def _pack_bf16_pair(a, b):
    lo = lax.bitcast_convert_type(a.astype(BF16).astype(F32), jnp.uint32) >> 16
    hi = lax.bitcast_convert_type(b.astype(BF16).astype(F32), jnp.uint32) & jnp.uint32(0xFFFF0000)
    return lo | hi


def _unpack_bf16_pair(word, index):
    bits = (word << 16) if index == 0 else (word & jnp.uint32(0xFFFF0000))
    return lax.bitcast_convert_type(bits, F32)


def _in_proj_kernel(x_ref, g_ref, w_ref, cos_ref, sin_ref,
                    z1_ref, z4_ref, z16_ref, gates_ref, h_ref, slab_ref, *, tn):
    tm = x_ref.shape[0]
    h_ref[...] = _rms(x_ref[...], g_ref[...]).astype(BF16)
    cos = cos_ref[...]
    sin = sin_ref[...]
    lane = lax.broadcasted_iota(jnp.int32, cos.shape, 1)
    first_half = lane < ROT_DIM // 2

    def project(col0):
        return jnp.dot(h_ref[...], w_ref[:, col0:col0 + tn], preferred_element_type=F32)

    for col0 in range(0, QKV_W, tn):
        acc = project(col0)
        for c in range(tn // V7X_LANES):
            lo_col = col0 + c * V7X_LANES
            cols = slice(lo_col, lo_col + V7X_LANES)
            slab = lo_col // V7X_LANES
            t = acc[:, c * V7X_LANES:(c + 1) * V7X_LANES]
            if lo_col < 2 * ATT_W:
                hi = pltpu.roll(t, HEAD_DIM - ROT_DIM // 2, axis=1)
                lo = pltpu.roll(t, ROT_DIM // 2, axis=1)
                t = t * cos + jnp.where(first_half, hi, lo) * sin
            z1_ref[:, cols] = t.astype(BF16)
            slab_ref[slab] = t
            for dil, ref in ((4, z4_ref), (16, z16_ref)):
                for r in range(dil):
                    src = slab_ref[slab, pl.ds(r, tm // dil, stride=dil), :]
                    ref[r, :, cols] = src.astype(BF16)
    for col0 in range(0, GATES_W, tn):
        gates_ref[:, col0:col0 + tn] = project(QKV_W + col0).astype(BF16)


def _in_proj(x, g, w, rope, *, batch, seq_len, tm, tn):
    t, d = x.shape
    n = w.shape[1]
    assert n == QKV_W + GATES_W and t % tm == 0 and seq_len % tm == 0
    assert QKV_W % tn == 0 and GATES_W % tn == 0 and tn % V7X_LANES == 0
    assert tm % (16 * V7X_BF16_SUBLANES) == 0
    tiles_per_seq = seq_len // tm

    def sorted_spec(dil):
        return pl.BlockSpec((None, dil, tm // dil, QKV_W),
                            lambda i: (i // tiles_per_seq, 0, i % tiles_per_seq, 0))

    rope_spec = pl.BlockSpec((tm, HEAD_DIM), lambda i: (i % tiles_per_seq, 0))
    z1, z4, z16, gates = pl.pallas_call(
        functools.partial(_in_proj_kernel, tn=tn),
        grid=(t // tm,),
        in_specs=[pl.BlockSpec((tm, d), lambda i: (i, 0)),
                  pl.BlockSpec((1, d), lambda i: (0, 0)),
                  _resident((d, n), lambda i: (0, 0)),
                  rope_spec, rope_spec],
        out_specs=[pl.BlockSpec((tm, QKV_W), lambda i: (i, 0)),
                   sorted_spec(4), sorted_spec(16),
                   pl.BlockSpec((tm, GATES_W), lambda i: (i, 0))],
        out_shape=[jax.ShapeDtypeStruct((t, QKV_W), BF16),
                   jax.ShapeDtypeStruct((batch, 4, seq_len // 4, QKV_W), BF16),
                   jax.ShapeDtypeStruct((batch, 16, seq_len // 16, QKV_W), BF16),
                   jax.ShapeDtypeStruct((t, GATES_W), BF16)],
        scratch_shapes=[pltpu.VMEM((tm, d), BF16),
                        pltpu.VMEM((QKV_W // V7X_LANES, tm, V7X_LANES), F32)],
        compiler_params=_params(("parallel",), 58),
        name="in_proj",
    )(x, g, w, *rope)
    return z1.reshape(batch, 1, seq_len, QKV_W), z4, z16, gates


def _rope_tables(seq_len):
    inv = jnp.float32(ROPE_THETA) ** (-jnp.arange(0, ROT_DIM, 2, dtype=F32) / ROT_DIM)
    ang = jnp.arange(seq_len, dtype=F32)[:, None] * inv[None, :]
    cos, sin = jnp.cos(ang), jnp.sin(ang)
    pad = HEAD_DIM - ROT_DIM
    cos_t = jnp.concatenate([cos, cos, jnp.ones((seq_len, pad), F32)], axis=1)
    sin_t = jnp.concatenate([-sin, sin, jnp.zeros((seq_len, pad), F32)], axis=1)
    return cos_t, sin_t


def _norm_matmul_kernel(x_ref, g_ref, w_ref, o_ref, h_ref):
    @pl.when(pl.program_id(1) == 0)
    def _():
        h_ref[...] = _rms(x_ref[...], g_ref[...]).astype(BF16)

    o_ref[...] = jnp.dot(h_ref[...], w_ref[...], preferred_element_type=F32).astype(BF16)


def _norm_matmul(x, g, w, *, tm, tn):
    rows, d = x.shape
    n = w.shape[1]
    assert rows % tm == 0 and n % tn == 0
    return pl.pallas_call(
        _norm_matmul_kernel,
        grid=(rows // tm, n // tn),
        in_specs=[pl.BlockSpec((tm, d), lambda i, j: (i, 0)),
                  pl.BlockSpec((1, d), lambda i, j: (0, 0)),
                  pl.BlockSpec((d, tn), lambda i, j: (0, j))],
        out_specs=pl.BlockSpec((tm, tn), lambda i, j: (i, j)),
        out_shape=jax.ShapeDtypeStruct((rows, n), BF16),
        scratch_shapes=[pltpu.VMEM((tm, d), BF16)],
        compiler_params=_params(("parallel", "arbitrary"), 32),
        name="norm_matmul",
    )(x, g, w)


def _branch_kernel(q_ref, kp_ref, k_ref, kn_ref, vp_ref, v_ref, vn_ref, o_ref, lse_ref,
                   *, res_len, dil):
    tq = q_ref.shape[0]
    r = pl.program_id(2)
    base = pl.program_id(1) * tq
    keys = jnp.concatenate([kp_ref[...], k_ref[...], kn_ref[...]], axis=0)
    vals = jnp.concatenate([vp_ref[...], v_ref[...], vn_ref[...]], axis=0)
    tb = ATT_Q_BLOCK
    nk = tb + 2 * BAND_HALF
    qi = lax.broadcasted_iota(jnp.int32, (tb, nk), 0)
    kj = lax.broadcasted_iota(jnp.int32, (tb, nk), 1)
    off = kj - qi
    band = (off >= 0) & (off <= 2 * BAND_HALF)
    lane = lax.broadcasted_iota(jnp.int32, (tb, V7X_LANES), 1)
    scale = HEAD_DIM ** -0.5
    for blk in range(tq // tb):
        kpos = base + blk * tb - BAND_HALF + kj
        valid = band & (kpos >= 0) & (kpos < res_len)
        qrows = slice(blk * tb, (blk + 1) * tb)
        krows = slice(blk * tb, blk * tb + nk)
        if dil == 1:
            out_rows = pl.ds(blk * tb, tb)
        else:
            out_rows = pl.ds(r + dil * blk * tb, tb, stride=dil)
        lse_all = jnp.zeros((tb, V7X_LANES), F32)
        for p in range(HEAD_PAIRS):
            pair = []
            for h in (2 * p, 2 * p + 1):
                cols = slice(h * HEAD_DIM, (h + 1) * HEAD_DIM)
                s = lax.dot_general(q_ref[qrows, cols], keys[krows, cols],
                                    (((1,), (1,)), ((), ())),
                                    preferred_element_type=F32) * scale
                s = jnp.where(valid, s, NEG)
                m = jnp.max(s, axis=-1, keepdims=True)
                e = jnp.exp(s - m)
                l = jnp.sum(e, axis=-1, keepdims=True)
                o = jnp.dot(e.astype(BF16), vals[krows, cols], preferred_element_type=F32)
                pair.append(o / l)
                lse_all = jnp.where(lane == h, m + jnp.log(l), lse_all)
            o_ref[p, out_rows, :] = _pack_bf16_pair(*pair)
        lse_ref[out_rows, :] = lse_all


def _dilated_branch(z, dil):
    batch, _, res_len, _ = z.shape
    seq_len = res_len * dil
    tq = min(ATT_MAX_Q_TILE, res_len)
    assert res_len % tq == 0 and tq % ATT_Q_BLOCK == 0 and tq % BAND_HALF == 0
    halo_per_tile = tq // BAND_HALF
    n_halo_blocks = res_len // BAND_HALF

    def main(part):
        return pl.BlockSpec((None, None, tq, ATT_W), lambda b, c, r: (b, r, c, part))

    def prev(part):
        return pl.BlockSpec((None, None, BAND_HALF, ATT_W),
                            lambda b, c, r: (b, r, jnp.maximum(c * halo_per_tile - 1, 0), part))

    def nxt(part):
        return pl.BlockSpec((None, None, BAND_HALF, ATT_W),
                            lambda b, c, r: (b, r, jnp.minimum((c + 1) * halo_per_tile,
                                                               n_halo_blocks - 1), part))

    return pl.pallas_call(
        functools.partial(_branch_kernel, res_len=res_len, dil=dil),
        grid=(batch, res_len // tq, dil),
        in_specs=[main(0), prev(1), main(1), nxt(1), prev(2), main(2), nxt(2)],
        out_specs=[pl.BlockSpec((None, HEAD_PAIRS, tq * dil, V7X_LANES),
                                lambda b, c, r: (b, 0, c, 0)),
                   pl.BlockSpec((None, tq * dil, V7X_LANES), lambda b, c, r: (b, c, 0))],
        out_shape=[jax.ShapeDtypeStruct((batch, HEAD_PAIRS, seq_len, V7X_LANES), jnp.uint32),
                   jax.ShapeDtypeStruct((batch, seq_len, V7X_LANES), F32)],
        compiler_params=_params(("parallel", "parallel", "arbitrary"), 48),
        name=f"dilated_branch_d{dil}",
    )(z, z, z, z, z, z, z)


def _mixer_out_kernel(o1_ref, o2_ref, o3_ref, l1_ref, l2_ref, l3_ref,
                      gb_ref, gc_ref, gh_ref, cp_ref, hp_ref, cn_ref, hn_ref,
                      cw_ref, ga_ref, gcv_ref, wo_ref, gpost_ref, x_ref, ex_ref, out_ref,
                      *, tiles_per_seq):
    i = pl.program_id(0)
    tm = x_ref.shape[0]

    u = gc_ref[...].astype(F32) * gh_ref[...].astype(F32)
    last = CONV_HALO_ROWS - 1
    u_prev = cp_ref[last:, :].astype(F32) * hp_ref[last:, :].astype(F32)
    u_next = cn_ref[:1, :].astype(F32) * hn_ref[:1, :].astype(F32)
    pos = i % tiles_per_seq
    u_prev = jnp.where(pos == 0, jnp.zeros_like(u_prev), u_prev)
    u_next = jnp.where(pos == tiles_per_seq - 1, jnp.zeros_like(u_next), u_next)
    row = lax.broadcasted_iota(jnp.int32, u.shape, 0)
    up = jnp.where(row == 0, u_prev, pltpu.roll(u, 1, axis=0))
    dn = jnp.where(row == tm - 1, u_next, pltpu.roll(u, tm - 1, axis=0))
    cw = cw_ref[...]
    y = up * cw[0:1, :] + u * cw[1:2, :] + dn * cw[2:3, :]
    cnv = gb_ref[...].astype(F32) * y
    cnv_n = _rms(cnv, gcv_ref[...])

    l1, l2, l3 = l1_ref[...], l2_ref[...], l3_ref[...]
    mx = jnp.maximum(jnp.maximum(l1, l2), l3)
    e1, e2, e3 = jnp.exp(l1 - mx), jnp.exp(l2 - mx), jnp.exp(l3 - mx)
    den = e1 + e2 + e3

    def spread(w):
        hi = w.astype(BF16)
        lo = (w - hi.astype(F32)).astype(BF16)
        return jnp.dot(jnp.concatenate([hi, lo], axis=-1), ex_ref[...],
                       preferred_element_type=F32)

    att = None
    for e, o_ref in ((e1, o1_ref), (e2, o2_ref), (e3, o3_ref)):
        o = jnp.concatenate([_unpack_bf16_pair(o_ref[p], idx)
                             for p in range(HEAD_PAIRS) for idx in range(2)], axis=-1)
        term = spread(e / den) * o
        att = term if att is None else att + term
    att_n = _rms(att, ga_ref[...])

    mixed = jnp.concatenate([att_n.astype(BF16), cnv_n.astype(BF16)], axis=-1)
    mix = jnp.dot(mixed, wo_ref[...], preferred_element_type=F32)
    out_ref[...] = x_ref[...] + _rms(mix, gpost_ref[...])


def _mixer_out(x, gates, branches, conv_w, g_attn, g_conv, w_o, g_post, *, seq_len, tm):
    t = x.shape[0]
    assert t % tm == 0 and seq_len % tm == 0 and tm % CONV_HALO_ROWS == 0
    tiles_per_seq = seq_len // tm
    halo_per_tile = tm // CONV_HALO_ROWS
    n_halo = t // CONV_HALO_ROWS

    def rows(width):
        return pl.BlockSpec((tm, width), lambda i: (i, 0))

    packed = pl.BlockSpec((None, HEAD_PAIRS, tm, V7X_LANES),
                          lambda i: (i // tiles_per_seq, 0, i % tiles_per_seq, 0))
    lse = pl.BlockSpec((None, tm, V7X_LANES),
                       lambda i: (i // tiles_per_seq, i % tiles_per_seq, 0))

    def gate(k):
        return pl.BlockSpec((tm, CONV_W), lambda i: (i, k))

    def halo_prev(k):
        return pl.BlockSpec((CONV_HALO_ROWS, CONV_W),
                            lambda i: (jnp.maximum(i * halo_per_tile - 1, 0), k))

    def halo_next(k):
        return pl.BlockSpec((CONV_HALO_ROWS, CONV_W),
                            lambda i: (jnp.minimum((i + 1) * halo_per_tile, n_halo - 1), k))

    def vec(width, nrows=1):
        return pl.BlockSpec((nrows, width), lambda i: (0, 0))

    head_of_col = jnp.arange(ATT_W, dtype=jnp.int32) // HEAD_DIM
    lane = jnp.arange(2 * V7X_LANES, dtype=jnp.int32) % V7X_LANES
    expand = (lane[:, None] == head_of_col[None, :]).astype(BF16)

    (o1, l1), (o2, l2), (o3, l3) = branches
    return pl.pallas_call(
        functools.partial(_mixer_out_kernel, tiles_per_seq=tiles_per_seq),
        grid=(t // tm,),
        in_specs=[packed, packed, packed, lse, lse, lse,
                  gate(0), gate(1), gate(2),
                  halo_prev(1), halo_prev(2), halo_next(1), halo_next(2),
                  vec(CONV_W, 3), vec(ATT_W), vec(CONV_W),
                  _resident((D_MODEL, D_MODEL), lambda i: (0, 0)),
                  vec(D_MODEL), rows(D_MODEL),
                  _resident((2 * V7X_LANES, ATT_W), lambda i: (0, 0))],
        out_specs=rows(D_MODEL),
        out_shape=jax.ShapeDtypeStruct((t, D_MODEL), F32),
        compiler_params=_params(("parallel",), 48),
        name="mixer_out",
    )(o1, o2, o3, l1, l2, l3, gates, gates, gates, gates, gates, gates, gates,
      conv_w, g_attn, g_conv, w_o, g_post, x, expand)


def _cross_attn_kernel(x_ref, gpre_ref, wq_ref, k_ref, v_ref, wo_ref, gpost_ref, out_ref):
    x = x_ref[...]
    xn = _rms(x, gpre_ref[...]).astype(BF16)
    q = jnp.dot(xn, wq_ref[...], preferred_element_type=F32).astype(BF16)
    scale = X_HEAD_DIM ** -0.5
    heads = []
    for h in range(X_HEADS):
        cols = slice(h * X_HEAD_DIM, (h + 1) * X_HEAD_DIM)
        s = lax.dot_general(q[:, cols], k_ref[:, cols], (((1,), (1,)), ((), ())),
                            preferred_element_type=F32) * scale
        m = jnp.max(s, axis=-1, keepdims=True)
        e = jnp.exp(s - m)
        p = e / jnp.sum(e, axis=-1, keepdims=True)
        o = jnp.dot(p.astype(BF16), v_ref[:, cols], preferred_element_type=F32)
        heads.append(o.astype(BF16))
    o = jnp.concatenate(heads, axis=-1)
    xa = jnp.dot(o, wo_ref[...], preferred_element_type=F32)
    out_ref[...] = x + _rms(xa, gpost_ref[...])


def _cross_attn(x, kv, g_pre, w_q, w_o, g_post, *, seq_len, tm):
    t = x.shape[0]
    assert t % tm == 0 and seq_len % tm == 0
    tiles_per_seq = seq_len // tm
    vec = pl.BlockSpec((1, D_MODEL), lambda i: (0, 0))
    rows = pl.BlockSpec((tm, D_MODEL), lambda i: (i, 0))
    return pl.pallas_call(
        _cross_attn_kernel,
        grid=(t // tm,),
        in_specs=[rows, vec,
                  _resident((D_MODEL, D_MODEL), lambda i: (0, 0)),
                  pl.BlockSpec((N_MEM, D_MODEL), lambda i: (i // tiles_per_seq, 0)),
                  pl.BlockSpec((N_MEM, D_MODEL), lambda i: (i // tiles_per_seq, 1)),
                  _resident((D_MODEL, D_MODEL), lambda i: (0, 0)),
                  vec],
        out_specs=rows,
        out_shape=jax.ShapeDtypeStruct((t, D_MODEL), F32),
        compiler_params=_params(("parallel",), 56),
        name="cross_attn",
    )(x, g_pre, w_q, kv, kv, w_o, g_post)


def _ffn_kernel(x_ref, gpre_ref, wg_ref, wu_ref, wd_ref, gpost_ref, out_ref, h_ref):
    j = pl.program_id(1)

    @pl.when(j == 0)
    def _():
        h_ref[...] = _rms(x_ref[...], gpre_ref[...]).astype(BF16)
        out_ref[...] = jnp.zeros_like(out_ref)

    rc = FFN_ROW_CHUNK
    for ch in range(x_ref.shape[0] // rc):
        rows = slice(ch * rc, (ch + 1) * rc)
        h = h_ref[rows, :]
        gate = jnp.dot(h, wg_ref[...], preferred_element_type=F32)
        up = jnp.dot(h, wu_ref[...], preferred_element_type=F32)
        act = (gate * jax.nn.sigmoid(gate) * up).astype(BF16)
        out_ref[rows, :] += jnp.dot(act, wd_ref[...], preferred_element_type=F32)

    @pl.when(j == pl.num_programs(1) - 1)
    def _():
        out_ref[...] = x_ref[...] + _rms(out_ref[...], gpost_ref[...])


def _ffn(x, g_pre, w_gate, w_up, w_down, g_post, *, tm, tf):
    t = x.shape[0]
    assert t % tm == 0 and D_FF % tf == 0
    vec = pl.BlockSpec((1, D_MODEL), lambda i, j: (0, 0))
    rows = pl.BlockSpec((tm, D_MODEL), lambda i, j: (i, 0))
    return pl.pallas_call(
        _ffn_kernel,
        grid=(t // tm, D_FF // tf),
        in_specs=[rows, vec,
                  pl.BlockSpec((D_MODEL, tf), lambda i, j: (0, j)),
                  pl.BlockSpec((D_MODEL, tf), lambda i, j: (0, j)),
                  pl.BlockSpec((tf, D_MODEL), lambda i, j: (j, 0)),
                  vec],
        out_specs=rows,
        out_shape=jax.ShapeDtypeStruct((t, D_MODEL), F32),
        scratch_shapes=[pltpu.VMEM((tm, D_MODEL), BF16)],
        compiler_params=_params(("parallel", "arbitrary"), 58),
        name="swiglu_ffn",
    )(x, g_pre, w_gate, w_up, w_down, g_post)


def _trunk(x, mem, layers, rope):
    batch, seq_len, _ = x.shape
    x = x.reshape(batch * seq_len, D_MODEL)
    mem = mem.reshape(batch * N_MEM, D_MODEL)
    for p in layers:
        *zs, gates = _in_proj(x, p["g_mix_pre"], p["w_in"], rope, batch=batch,
                              seq_len=seq_len, tm=IN_PROJ_ROW_TILE, tn=COL_TILE)
        branches = [_dilated_branch(z, dil) for z, dil in zip(zs, DILATIONS)]
        x = _mixer_out(x, gates, branches, p["conv_w"], p["g_attn_out"], p["g_conv_out"],
                       p["w_o"], p["g_mix_post"], seq_len=seq_len, tm=ROW_TILE)
        kv = _norm_matmul(mem, p["g_mem"], p["w_xkv"], tm=N_MEM, tn=COL_TILE)
        x = _cross_attn(x, kv, p["g_x_pre"], p["w_xq"], p["w_xo"], p["g_x_post"],
                        seq_len=seq_len, tm=ROW_TILE)
        x = _ffn(x, p["g_ffn_pre"], p["w_gate"], p["w_up"], p["w_down"], p["g_ffn_post"],
                 tm=FFN_ROW_TILE, tf=FFN_COL_TILE)
    return x.reshape(batch, seq_len, D_MODEL)


def kernel(x_prompt, x_sample, mem_prompt, mem_sample, g_mix_pre, w_in, conv_w, g_attn_out, g_conv_out, w_o, g_mix_post, g_x_pre, g_mem, w_xq, w_xk, w_xv, w_xo, g_x_post, g_ffn_pre, w_gate, w_up, w_down, g_ffn_post):
    depth = w_in.shape[0]
    layers = []
    for l in range(depth):
        layers.append({
            "g_mix_pre": g_mix_pre[l][None, :], "w_in": w_in[l].astype(BF16),
            "conv_w": conv_w[l], "g_attn_out": g_attn_out[l][None, :],
            "g_conv_out": g_conv_out[l][None, :], "w_o": w_o[l].astype(BF16),
            "g_mix_post": g_mix_post[l][None, :], "g_x_pre": g_x_pre[l][None, :],
            "g_mem": g_mem[l][None, :], "w_xq": w_xq[l].astype(BF16),
            "w_xkv": jnp.concatenate([w_xk[l], w_xv[l]], axis=1).astype(BF16),
            "w_xo": w_xo[l].astype(BF16), "g_x_post": g_x_post[l][None, :],
            "g_ffn_pre": g_ffn_pre[l][None, :], "w_gate": w_gate[l].astype(BF16),
            "w_up": w_up[l].astype(BF16), "w_down": w_down[l].astype(BF16),
            "g_ffn_post": g_ffn_post[l][None, :],
        })
    outs = []
    for x, mem in ((x_prompt, mem_prompt), (x_sample, mem_sample)):
        rope = _rope_tables(x.shape[1])
        outs.append(_trunk(x, mem, layers, rope))
    return tuple(outs)
```

```python
import functools

import jax
import jax.numpy as jnp
from jax import lax
from jax.experimental import pallas as pl
from jax.experimental.pallas import tpu as pltpu

F32 = jnp.float32
BF16 = jnp.bfloat16

D_MODEL = 2048
HEAD_DIM = 128
N_ATT_HEADS = 12
ATT_W = N_ATT_HEADS * HEAD_DIM
QKV_W = 3 * ATT_W
CONV_W = D_MODEL - ATT_W
GATES_W = 3 * CONV_W
DILATIONS = (1, 4, 16)
BAND_HALF = 64
ROT_DIM = HEAD_DIM // 4
ROPE_THETA = 500000.0
N_MEM = 256
X_HEADS = 4
X_HEAD_DIM = D_MODEL // X_HEADS
D_FF = 5632
EPS = 1e-6
NEG = -1e30

V7X_LANES = 128
V7X_BF16_SUBLANES = 16
V7X_VMEM_BYTES = 64 * 1024 * 1024
MIB = 1024 * 1024

ROW_TILE = 512
IN_PROJ_ROW_TILE = 256
IN_PROJ_COL_TILE = ATT_W
FFN_ROW_TILE = 1024
COL_TILE = 512
FFN_COL_TILE = 512
FFN_ROW_CHUNK = 256
ATT_Q_BLOCK = 128
ATT_MAX_Q_TILE = 256
CONV_HALO_ROWS = V7X_BF16_SUBLANES
HEAD_PAIRS = N_ATT_HEADS // 2


def _params(semantics, vmem_mib):
    assert vmem_mib * MIB < V7X_VMEM_BYTES
    return pltpu.CompilerParams(dimension_semantics=semantics,
                                vmem_limit_bytes=vmem_mib * MIB)


def _resident(shape, index_map):
    return pl.BlockSpec(shape, index_map, pipeline_mode=pl.Buffered(1))


def _rms(x, g):
    ms = jnp.mean(x * x, axis=-1, keepdims=True)
    return (x * lax.rsqrt(ms + EPS)) * g


def _pack_bf16_pair(a, b):
    lo = lax.bitcast_convert_type(a.astype(BF16).astype(F32), jnp.uint32) >> 16
    hi = lax.bitcast_convert_type(b.astype(BF16).astype(F32), jnp.uint32) & jnp.uint32(0xFFFF0000)
    return lo | hi


def _unpack_bf16_pair(word, index):
    bits = (word << 16) if index == 0 else (word & jnp.uint32(0xFFFF0000))
    return lax.bitcast_convert_type(bits, F32)


def _in_proj_kernel(x_ref, g_ref, w_ref, cos_ref, sin_ref,
                    z1_ref, z4_ref, z16_ref, gates_ref, h_ref, slab_ref, *, tn):
    tm = x_ref.shape[0]
    h_ref[...] = _rms(x_ref[...], g_ref[...]).astype(BF16)
    cos = cos_ref[...]
    sin = sin_ref[...]
    lane = lax.broadcasted_iota(jnp.int32, cos.shape, 1)
    first_half = lane < ROT_DIM // 2

    def project(col0):
        return jnp.dot(h_ref[...], w_ref[:, col0:col0 + tn], preferred_element_type=F32)

    for col0 in range(0, QKV_W, tn):
        acc = project(col0)
        for c in range(tn // V7X_LANES):
            lo_col = col0 + c * V7X_LANES
            cols = slice(lo_col, lo_col + V7X_LANES)
            slab = lo_col // V7X_LANES
            t = acc[:, c * V7X_LANES:(c + 1) * V7X_LANES]
            if lo_col < 2 * ATT_W:
                hi = pltpu.roll(t, HEAD_DIM - ROT_DIM // 2, axis=1)
                lo = pltpu.roll(t, ROT_DIM // 2, axis=1)
                t = t * cos + jnp.where(first_half, hi, lo) * sin
            z1_ref[:, cols] = t.astype(BF16)
            slab_ref[slab] = t
            for dil, ref in ((4, z4_ref), (16, z16_ref)):
                for r in range(dil):
                    src = slab_ref[slab, pl.ds(r, tm // dil, stride=dil), :]
                    ref[r, :, cols] = src.astype(BF16)
    for col0 in range(0, GATES_W, tn):
        gates_ref[:, col0:col0 + tn] = project(QKV_W + col0).astype(BF16)


def _in_proj(x, g, w, rope, *, batch, seq_len, tm, tn):
    t, d = x.shape
    n = w.shape[1]
    assert n == QKV_W + GATES_W and t % tm == 0 and seq_len % tm == 0
    assert QKV_W % tn == 0 and GATES_W % tn == 0 and tn % V7X_LANES == 0
    assert tm % (16 * V7X_BF16_SUBLANES) == 0
    tiles_per_seq = seq_len // tm

    def sorted_spec(dil):
        return pl.BlockSpec((None, dil, tm // dil, QKV_W),
                            lambda i: (i // tiles_per_seq, 0, i % tiles_per_seq, 0))

    rope_spec = pl.BlockSpec((tm, HEAD_DIM), lambda i: (i % tiles_per_seq, 0))
    z1, z4, z16, gates = pl.pallas_call(
        functools.partial(_in_proj_kernel, tn=tn),
        grid=(t // tm,),
        in_specs=[pl.BlockSpec((tm, d), lambda i: (i, 0)),
                  pl.BlockSpec((1, d), lambda i: (0, 0)),
                  _resident((d, n), lambda i: (0, 0)),
                  rope_spec, rope_spec],
        out_specs=[pl.BlockSpec((tm, QKV_W), lambda i: (i, 0)),
                   sorted_spec(4), sorted_spec(16),
                   pl.BlockSpec((tm, GATES_W), lambda i: (i, 0))],
        out_shape=[jax.ShapeDtypeStruct((t, QKV_W), BF16),
                   jax.ShapeDtypeStruct((batch, 4, seq_len // 4, QKV_W), BF16),
                   jax.ShapeDtypeStruct((batch, 16, seq_len // 16, QKV_W), BF16),
                   jax.ShapeDtypeStruct((t, GATES_W), BF16)],
        scratch_shapes=[pltpu.VMEM((tm, d), BF16),
                        pltpu.VMEM((QKV_W // V7X_LANES, tm, V7X_LANES), F32)],
        compiler_params=_params(("parallel",), 58),
        name="in_proj",
    )(x, g, w, *rope)
    return z1.reshape(batch, 1, seq_len, QKV_W), z4, z16, gates


def _rope_tables(seq_len):
    inv = jnp.float32(ROPE_THETA) ** (-jnp.arange(0, ROT_DIM, 2, dtype=F32) / ROT_DIM)
    ang = jnp.arange(seq_len, dtype=F32)[:, None] * inv[None, :]
    cos, sin = jnp.cos(ang), jnp.sin(ang)
    pad = HEAD_DIM - ROT_DIM
    cos_t = jnp.concatenate([cos, cos, jnp.ones((seq_len, pad), F32)], axis=1)
    sin_t = jnp.concatenate([-sin, sin, jnp.zeros((seq_len, pad), F32)], axis=1)
    return cos_t, sin_t


def _norm_matmul_kernel(x_ref, g_ref, w_ref, o_ref, h_ref):
    @pl.when(pl.program_id(1) == 0)
    def _():
        h_ref[...] = _rms(x_ref[...], g_ref[...]).astype(BF16)

    o_ref[...] = jnp.dot(h_ref[...], w_ref[...], preferred_element_type=F32).astype(BF16)


def _norm_matmul(x, g, w, *, tm, tn):
    rows, d = x.shape
    n = w.shape[1]
    assert rows % tm == 0 and n % tn == 0
    return pl.pallas_call(
        _norm_matmul_kernel,
        grid=(rows // tm, n // tn),
        in_specs=[pl.BlockSpec((tm, d), lambda i, j: (i, 0)),
                  pl.BlockSpec((1, d), lambda i, j: (0, 0)),
                  pl.BlockSpec((d, tn), lambda i, j: (0, j))],
        out_specs=pl.BlockSpec((tm, tn), lambda i, j: (i, j)),
        out_shape=jax.ShapeDtypeStruct((rows, n), BF16),
        scratch_shapes=[pltpu.VMEM((tm, d), BF16)],
        compiler_params=_params(("parallel", "arbitrary"), 32),
        name="norm_matmul",
    )(x, g, w)


def _branch_kernel(q_ref, kp_ref, k_ref, kn_ref, vp_ref, v_ref, vn_ref, o_ref, lse_ref,
                   *, res_len, dil):
    tq = q_ref.shape[0]
    r = pl.program_id(2)
    base = pl.program_id(1) * tq
    keys = jnp.concatenate([kp_ref[...], k_ref[...], kn_ref[...]], axis=0)
    vals = jnp.concatenate([vp_ref[...], v_ref[...], vn_ref[...]], axis=0)
    tb = ATT_Q_BLOCK
    nk = tb + 2 * BAND_HALF
    qi = lax.broadcasted_iota(jnp.int32, (tb, nk), 0)
    kj = lax.broadcasted_iota(jnp.int32, (tb, nk), 1)
    off = kj - qi
    band = (off >= 0) & (off <= 2 * BAND_HALF)
    lane = lax.broadcasted_iota(jnp.int32, (tb, V7X_LANES), 1)
    scale = HEAD_DIM ** -0.5
    exp2_scale = scale * 1.4426950408889634
    ones = jnp.ones((nk, HEAD_DIM), BF16)
    for blk in range(tq // tb):
        kpos = base + blk * tb - BAND_HALF + kj
        valid = band & (kpos >= 0) & (kpos < res_len)
        qrows = slice(blk * tb, (blk + 1) * tb)
        krows = slice(blk * tb, blk * tb + nk)
        if dil == 1:
            out_rows = pl.ds(blk * tb, tb)
        else:
            out_rows = pl.ds(r + dil * blk * tb, tb, stride=dil)
        lse_all = jnp.zeros((tb, V7X_LANES), F32)
        for p in range(HEAD_PAIRS):
            pair = []
            for h in (2 * p, 2 * p + 1):
                cols = slice(h * HEAD_DIM, (h + 1) * HEAD_DIM)
                s = lax.dot_general(q_ref[qrows, cols], keys[krows, cols],
                                    (((1,), (1,)), ((), ())),
                                    preferred_element_type=F32)
                s = jnp.where(valid, s, NEG)
                m = jnp.max(s, axis=-1, keepdims=True)
                e = jnp.exp2((s - m) * exp2_scale)
                v_ones = jnp.concatenate([vals[krows, cols], ones], axis=-1)
                ol = jnp.dot(e.astype(BF16), v_ones, preferred_element_type=F32)
                l = ol[:, HEAD_DIM:]
                pair.append(ol[:, :HEAD_DIM] / l)
                lse_all = jnp.where(lane == h, m * scale + jnp.log(l), lse_all)
            o_ref[p, out_rows, :] = _pack_bf16_pair(*pair)
        lse_ref[out_rows, :] = lse_all


def _dilated_branch(z, dil):
    batch, _, res_len, _ = z.shape
    seq_len = res_len * dil
    tq = min(ATT_MAX_Q_TILE, res_len)
    assert res_len % tq == 0 and tq % ATT_Q_BLOCK == 0 and tq % BAND_HALF == 0
    halo_per_tile = tq // BAND_HALF
    n_halo_blocks = res_len // BAND_HALF

    def main(part):
        return pl.BlockSpec((None, None, tq, ATT_W), lambda b, c, r: (b, r, c, part))

    def prev(part):
        return pl.BlockSpec((None, None, BAND_HALF, ATT_W),
                            lambda b, c, r: (b, r, jnp.maximum(c * halo_per_tile - 1, 0), part))

    def nxt(part):
        return pl.BlockSpec((None, None, BAND_HALF, ATT_W),
                            lambda b, c, r: (b, r, jnp.minimum((c + 1) * halo_per_tile,
                                                               n_halo_blocks - 1), part))

    return pl.pallas_call(
        functools.partial(_branch_kernel, res_len=res_len, dil=dil),
        grid=(batch, res_len // tq, dil),
        in_specs=[main(0), prev(1), main(1), nxt(1), prev(2), main(2), nxt(2)],
        out_specs=[pl.BlockSpec((None, HEAD_PAIRS, tq * dil, V7X_LANES),
                                lambda b, c, r: (b, 0, c, 0)),
                   pl.BlockSpec((None, tq * dil, V7X_LANES), lambda b, c, r: (b, c, 0))],
        out_shape=[jax.ShapeDtypeStruct((batch, HEAD_PAIRS, seq_len, V7X_LANES), jnp.uint32),
                   jax.ShapeDtypeStruct((batch, seq_len, V7X_LANES), F32)],
        compiler_params=_params(("parallel", "parallel", "arbitrary"), 48),
        name=f"dilated_branch_d{dil}",
    )(z, z, z, z, z, z, z)


def _mixer_out_kernel(o1_ref, o2_ref, o3_ref, l1_ref, l2_ref, l3_ref,
                      gb_ref, gc_ref, gh_ref, cp_ref, hp_ref, cn_ref, hn_ref,
                      cw_ref, ga_ref, gcv_ref, wo_ref, gpost_ref, x_ref, ex_ref, out_ref,
                      *, tiles_per_seq):
    i = pl.program_id(0)
    tm = x_ref.shape[0]

    u = gc_ref[...].astype(F32) * gh_ref[...].astype(F32)
    last = CONV_HALO_ROWS - 1
    u_prev = cp_ref[last:, :].astype(F32) * hp_ref[last:, :].astype(F32)
    u_next = cn_ref[:1, :].astype(F32) * hn_ref[:1, :].astype(F32)
    pos = i % tiles_per_seq
    u_prev = jnp.where(pos == 0, jnp.zeros_like(u_prev), u_prev)
    u_next = jnp.where(pos == tiles_per_seq - 1, jnp.zeros_like(u_next), u_next)
    row = lax.broadcasted_iota(jnp.int32, u.shape, 0)
    up = jnp.where(row == 0, u_prev, pltpu.roll(u, 1, axis=0))
    dn = jnp.where(row == tm - 1, u_next, pltpu.roll(u, tm - 1, axis=0))
    cw = cw_ref[...]
    y = up * cw[0:1, :] + u * cw[1:2, :] + dn * cw[2:3, :]
    cnv = gb_ref[...].astype(F32) * y
    cnv_n = _rms(cnv, gcv_ref[...])

    l1, l2, l3 = l1_ref[...], l2_ref[...], l3_ref[...]
    mx = jnp.maximum(jnp.maximum(l1, l2), l3)
    e1, e2, e3 = jnp.exp(l1 - mx), jnp.exp(l2 - mx), jnp.exp(l3 - mx)
    den = e1 + e2 + e3

    def spread(w):
        hi = w.astype(BF16)
        lo = (w - hi.astype(F32)).astype(BF16)
        return jnp.dot(jnp.concatenate([hi, lo], axis=-1), ex_ref[...],
                       preferred_element_type=F32)

    att = None
    for e, o_ref in ((e1, o1_ref), (e2, o2_ref), (e3, o3_ref)):
        o = jnp.concatenate([_unpack_bf16_pair(o_ref[p], idx)
                             for p in range(HEAD_PAIRS) for idx in range(2)], axis=-1)
        term = spread(e / den) * o
        att = term if att is None else att + term
    att_n = _rms(att, ga_ref[...])

    mixed = jnp.concatenate([att_n.astype(BF16), cnv_n.astype(BF16)], axis=-1)
    mix = jnp.dot(mixed, wo_ref[...], preferred_element_type=F32)
    out_ref[...] = x_ref[...] + _rms(mix, gpost_ref[...])


def _mixer_out(x, gates, branches, conv_w, g_attn, g_conv, w_o, g_post, *, seq_len, tm):
    t = x.shape[0]
    assert t % tm == 0 and seq_len % tm == 0 and tm % CONV_HALO_ROWS == 0
    tiles_per_seq = seq_len // tm
    halo_per_tile = tm // CONV_HALO_ROWS
    n_halo = t // CONV_HALO_ROWS

    def rows(width):
        return pl.BlockSpec((tm, width), lambda i: (i, 0))

    packed = pl.BlockSpec((None, HEAD_PAIRS, tm, V7X_LANES),
                          lambda i: (i // tiles_per_seq, 0, i % tiles_per_seq, 0))
    lse = pl.BlockSpec((None, tm, V7X_LANES),
                       lambda i: (i // tiles_per_seq, i % tiles_per_seq, 0))

    def gate(k):
        return pl.BlockSpec((tm, CONV_W), lambda i: (i, k))

    def halo_prev(k):
        return pl.BlockSpec((CONV_HALO_ROWS, CONV_W),
                            lambda i: (jnp.maximum(i * halo_per_tile - 1, 0), k))

    def halo_next(k):
        return pl.BlockSpec((CONV_HALO_ROWS, CONV_W),
                            lambda i: (jnp.minimum((i + 1) * halo_per_tile, n_halo - 1), k))

    def vec(width, nrows=1):
        return pl.BlockSpec((nrows, width), lambda i: (0, 0))

    head_of_col = jnp.arange(ATT_W, dtype=jnp.int32) // HEAD_DIM
    lane = jnp.arange(2 * V7X_LANES, dtype=jnp.int32) % V7X_LANES
    expand = (lane[:, None] == head_of_col[None, :]).astype(BF16)

    (o1, l1), (o2, l2), (o3, l3) = branches
    return pl.pallas_call(
        functools.partial(_mixer_out_kernel, tiles_per_seq=tiles_per_seq),
        grid=(t // tm,),
        in_specs=[packed, packed, packed, lse, lse, lse,
                  gate(0), gate(1), gate(2),
                  halo_prev(1), halo_prev(2), halo_next(1), halo_next(2),
                  vec(CONV_W, 3), vec(ATT_W), vec(CONV_W),
                  _resident((D_MODEL, D_MODEL), lambda i: (0, 0)),
                  vec(D_MODEL), rows(D_MODEL),
                  _resident((2 * V7X_LANES, ATT_W), lambda i: (0, 0))],
        out_specs=rows(D_MODEL),
        out_shape=jax.ShapeDtypeStruct((t, D_MODEL), F32),
        compiler_params=_params(("parallel",), 48),
        name="mixer_out",
    )(o1, o2, o3, l1, l2, l3, gates, gates, gates, gates, gates, gates, gates,
      conv_w, g_attn, g_conv, w_o, g_post, x, expand)


def _cross_attn_kernel(x_ref, gpre_ref, wq_ref, k_ref, v_ref, wo_ref, gpost_ref, out_ref):
    x = x_ref[...]
    xn = _rms(x, gpre_ref[...]).astype(BF16)
    q = jnp.dot(xn, wq_ref[...], preferred_element_type=F32).astype(BF16)
    scale = X_HEAD_DIM ** -0.5
    heads = []
    for h in range(X_HEADS):
        cols = slice(h * X_HEAD_DIM, (h + 1) * X_HEAD_DIM)
        s = lax.dot_general(q[:, cols], k_ref[:, cols], (((1,), (1,)), ((), ())),
                            preferred_element_type=F32) * scale
        m = jnp.max(s, axis=-1, keepdims=True)
        e = jnp.exp(s - m)
        p = e / jnp.sum(e, axis=-1, keepdims=True)
        o = jnp.dot(p.astype(BF16), v_ref[:, cols], preferred_element_type=F32)
        heads.append(o.astype(BF16))
    o = jnp.concatenate(heads, axis=-1)
    xa = jnp.dot(o, wo_ref[...], preferred_element_type=F32)
    out_ref[...] = x + _rms(xa, gpost_ref[...])


def _cross_attn(x, kv, g_pre, w_q, w_o, g_post, *, seq_len, tm):
    t = x.shape[0]
    assert t % tm == 0 and seq_len % tm == 0
    tiles_per_seq = seq_len // tm
    vec = pl.BlockSpec((1, D_MODEL), lambda i: (0, 0))
    rows = pl.BlockSpec((tm, D_MODEL), lambda i: (i, 0))
    return pl.pallas_call(
        _cross_attn_kernel,
        grid=(t // tm,),
        in_specs=[rows, vec,
                  _resident((D_MODEL, D_MODEL), lambda i: (0, 0)),
                  pl.BlockSpec((N_MEM, D_MODEL), lambda i: (i // tiles_per_seq, 0)),
                  pl.BlockSpec((N_MEM, D_MODEL), lambda i: (i // tiles_per_seq, 1)),
                  _resident((D_MODEL, D_MODEL), lambda i: (0, 0)),
                  vec],
        out_specs=rows,
        out_shape=jax.ShapeDtypeStruct((t, D_MODEL), F32),
        compiler_params=_params(("parallel",), 56),
        name="cross_attn",
    )(x, g_pre, w_q, kv, kv, w_o, g_post)


def _ffn_kernel(x_ref, gpre_ref, wg_ref, wu_ref, wd_ref, gpost_ref, out_ref, h_ref):
    j = pl.program_id(1)

    @pl.when(j == 0)
    def _():
        h_ref[...] = _rms(x_ref[...], gpre_ref[...]).astype(BF16)
        out_ref[...] = jnp.zeros_like(out_ref)

    rc = FFN_ROW_CHUNK
    for ch in range(x_ref.shape[0] // rc):
        rows = slice(ch * rc, (ch + 1) * rc)
        h = h_ref[rows, :]
        gate = jnp.dot(h, wg_ref[...], preferred_element_type=F32)
        up = jnp.dot(h, wu_ref[...], preferred_element_type=F32)
        act = (gate * jax.nn.sigmoid(gate) * up).astype(BF16)
        out_ref[rows, :] += jnp.dot(act, wd_ref[...], preferred_element_type=F32)

    @pl.when(j == pl.num_programs(1) - 1)
    def _():
        out_ref[...] = x_ref[...] + _rms(out_ref[...], gpost_ref[...])


def _ffn(x, g_pre, w_gate, w_up, w_down, g_post, *, tm, tf):
    t = x.shape[0]
    assert t % tm == 0 and D_FF % tf == 0
    vec = pl.BlockSpec((1, D_MODEL), lambda i, j: (0, 0))
    rows = pl.BlockSpec((tm, D_MODEL), lambda i, j: (i, 0))
    return pl.pallas_call(
        _ffn_kernel,
        grid=(t // tm, D_FF // tf),
        in_specs=[rows, vec,
                  pl.BlockSpec((D_MODEL, tf), lambda i, j: (0, j)),
                  pl.BlockSpec((D_MODEL, tf), lambda i, j: (0, j)),
                  pl.BlockSpec((tf, D_MODEL), lambda i, j: (j, 0)),
                  vec],
        out_specs=rows,
        out_shape=jax.ShapeDtypeStruct((t, D_MODEL), F32),
        scratch_shapes=[pltpu.VMEM((tm, D_MODEL), BF16)],
        compiler_params=_params(("parallel", "arbitrary"), 58),
        name="swiglu_ffn",
    )(x, g_pre, w_gate, w_up, w_down, g_post)


def _trunk(x, mem, layers, rope):
    batch, seq_len, _ = x.shape
    x = x.reshape(batch * seq_len, D_MODEL)
    mem = mem.reshape(batch * N_MEM, D_MODEL)
    for p in layers:
        *zs, gates = _in_proj(x, p["g_mix_pre"], p["w_in"], rope, batch=batch,
                              seq_len=seq_len, tm=IN_PROJ_ROW_TILE, tn=IN_PROJ_COL_TILE)
        branches = [_dilated_branch(z, dil) for z, dil in zip(zs, DILATIONS)]
        x = _mixer_out(x, gates, branches, p["conv_w"], p["g_attn_out"], p["g_conv_out"],
                       p["w_o"], p["g_mix_post"], seq_len=seq_len, tm=ROW_TILE)
        kv = _norm_matmul(mem, p["g_mem"], p["w_xkv"], tm=N_MEM, tn=COL_TILE)
        x = _cross_attn(x, kv, p["g_x_pre"], p["w_xq"], p["w_xo"], p["g_x_post"],
                        seq_len=seq_len, tm=ROW_TILE)
        x = _ffn(x, p["g_ffn_pre"], p["w_gate"], p["w_up"], p["w_down"], p["g_ffn_post"],
                 tm=FFN_ROW_TILE, tf=FFN_COL_TILE)
    return x.reshape(batch, seq_len, D_MODEL)


def kernel(x_prompt, x_sample, mem_prompt, mem_sample, g_mix_pre, w_in, conv_w, g_attn_out, g_conv_out, w_o, g_mix_post, g_x_pre, g_mem, w_xq, w_xk, w_xv, w_xo, g_x_post, g_ffn_pre, w_gate, w_up, w_down, g_ffn_post):
    depth = w_in.shape[0]
    layers = []
    for l in range(depth):
        layers.append({
            "g_mix_pre": g_mix_pre[l][None, :], "w_in": w_in[l].astype(BF16),
            "conv_w": conv_w[l], "g_attn_out": g_attn_out[l][None, :],
            "g_conv_out": g_conv_out[l][None, :], "w_o": w_o[l].astype(BF16),
            "g_mix_post": g_mix_post[l][None, :], "g_x_pre": g_x_pre[l][None, :],
            "g_mem": g_mem[l][None, :], "w_xq": w_xq[l].astype(BF16),
            "w_xkv": jnp.concatenate([w_xk[l], w_xv[l]], axis=1).astype(BF16),
            "w_xo": w_xo[l].astype(BF16), "g_x_post": g_x_post[l][None, :],
            "g_ffn_pre": g_ffn_pre[l][None, :], "w_gate": w_gate[l].astype(BF16),
            "w_up": w_up[l].astype(BF16), "w_down": w_down[l].astype(BF16),
            "g_ffn_post": g_ffn_post[l][None, :],
        })
    outs = []
    for x, mem in ((x_prompt, mem_prompt), (x_sample, mem_sample)):
        rope = _rope_tables(x.shape[1])
        outs.append(_trunk(x, mem, layers, rope))
    return tuple(outs)
```

```python
import functools

import jax
import jax.numpy as jnp
from jax import lax
from jax.experimental import pallas as pl
from jax.experimental.pallas import tpu as pltpu

F32 = jnp.float32
BF16 = jnp.bfloat16

D_MODEL = 2048
HEAD_DIM = 128
N_ATT_HEADS = 12
ATT_W = N_ATT_HEADS * HEAD_DIM
QKV_W = 3 * ATT_W
CONV_W = D_MODEL - ATT_W
GATES_W = 3 * CONV_W
DILATIONS = (1, 4, 16)
BAND_HALF = 64
ROT_DIM = HEAD_DIM // 4
ROPE_THETA = 500000.0
N_MEM = 256
X_HEADS = 4
X_HEAD_DIM = D_MODEL // X_HEADS
D_FF = 5632
EPS = 1e-6
NEG = -1e30

V7X_LANES = 128
V7X_BF16_SUBLANES = 16
V7X_VMEM_BYTES = 64 * 1024 * 1024
MIB = 1024 * 1024

ROW_TILE = 512
IN_PROJ_ROW_TILE = 256
IN_PROJ_COL_TILE = ATT_W
IN_PROJ_SORT_RING = 12
FFN_ROW_TILE = 1024
KV_ROW_TILE = 1024
KV_COL_TILE = 1024
FFN_COL_TILE = 512
FFN_ROW_CHUNK = 256
ATT_Q_BLOCK = 128
ATT_MAX_Q_TILE = 256
CONV_HALO_ROWS = V7X_BF16_SUBLANES
HEAD_PAIRS = N_ATT_HEADS // 2


def _params(semantics, vmem_mib):
    assert vmem_mib * MIB < V7X_VMEM_BYTES
    return pltpu.CompilerParams(dimension_semantics=semantics,
                                vmem_limit_bytes=vmem_mib * MIB)


def _resident(shape, index_map):
    return pl.BlockSpec(shape, index_map, pipeline_mode=pl.Buffered(1))


def _rms(x, g):
    ms = jnp.mean(x * x, axis=-1, keepdims=True)
    return (x * lax.rsqrt(ms + EPS)) * g


def _pack_bf16_pair(a, b):
    lo = lax.bitcast_convert_type(a.astype(BF16).astype(F32), jnp.uint32) >> 16
    hi = lax.bitcast_convert_type(b.astype(BF16).astype(F32), jnp.uint32) & jnp.uint32(0xFFFF0000)
    return lo | hi


def _unpack_bf16_pair(word, index):
    bits = (word << 16) if index == 0 else (word & jnp.uint32(0xFFFF0000))
    return lax.bitcast_convert_type(bits, F32)


def _in_proj_kernel(x_ref, g_ref, w_ref, cos_ref, sin_ref,
                    z1_ref, z4_ref, z16_ref, gates_ref, h_ref, slab_ref, slab4_ref, *, tn):
    tm = x_ref.shape[0]
    h_ref[...] = _rms(x_ref[...], g_ref[...]).astype(BF16)
    cos = cos_ref[...]
    sin = sin_ref[...]
    lane = lax.broadcasted_iota(jnp.int32, cos.shape, 1)
    first_half = lane < ROT_DIM // 2

    def project(col0):
        return jnp.dot(h_ref[...], w_ref[:, col0:col0 + tn], preferred_element_type=F32)

    for col0 in range(0, QKV_W, tn):
        acc = project(col0)
        for c in range(tn // V7X_LANES):
            lo_col = col0 + c * V7X_LANES
            cols = slice(lo_col, lo_col + V7X_LANES)
            slab = lo_col // V7X_LANES
            t = acc[:, c * V7X_LANES:(c + 1) * V7X_LANES]
            if lo_col < 2 * ATT_W:
                hi = pltpu.roll(t, HEAD_DIM - ROT_DIM // 2, axis=1)
                lo = pltpu.roll(t, ROT_DIM // 2, axis=1)
                t = t * cos + jnp.where(first_half, hi, lo) * sin
            z1_ref[:, cols] = t.astype(BF16)
            slab_ref[slab] = t
            n4, n16 = tm // 4, tm // 16
            ring = slab % slab4_ref.shape[0]
            for r in range(4):
                s4 = slab_ref[slab, pl.ds(r, n4, stride=4), :]
                z4_ref[r, :, cols] = s4.astype(BF16)
                slab4_ref[ring, r * n4:(r + 1) * n4, :] = s4
            for r in range(4):
                for t4 in range(4):
                    s16 = slab4_ref[ring, pl.ds(r * n4 + t4, n16, stride=4), :]
                    z16_ref[r + 4 * t4, :, cols] = s16.astype(BF16)
    for col0 in range(0, GATES_W, tn):
        gates_ref[:, col0:col0 + tn] = project(QKV_W + col0).astype(BF16)


def _in_proj(x, g, w, rope, *, batch, seq_len, tm, tn):
    t, d = x.shape
    n = w.shape[1]
    assert n == QKV_W + GATES_W and t % tm == 0 and seq_len % tm == 0
    assert QKV_W % tn == 0 and GATES_W % tn == 0 and tn % V7X_LANES == 0
    assert tm % (16 * V7X_BF16_SUBLANES) == 0
    tiles_per_seq = seq_len // tm

    def sorted_spec(dil):
        return pl.BlockSpec((None, dil, tm // dil, QKV_W),
                            lambda i: (i // tiles_per_seq, 0, i % tiles_per_seq, 0))

    rope_spec = pl.BlockSpec((tm, HEAD_DIM), lambda i: (i % tiles_per_seq, 0))
    z1, z4, z16, gates = pl.pallas_call(
        functools.partial(_in_proj_kernel, tn=tn),
        grid=(t // tm,),
        in_specs=[pl.BlockSpec((tm, d), lambda i: (i, 0)),
                  pl.BlockSpec((1, d), lambda i: (0, 0)),
                  _resident((d, n), lambda i: (0, 0)),
                  rope_spec, rope_spec],
        out_specs=[pl.BlockSpec((tm, QKV_W), lambda i: (i, 0)),
                   sorted_spec(4), sorted_spec(16),
                   pl.BlockSpec((tm, GATES_W), lambda i: (i, 0))],
        out_shape=[jax.ShapeDtypeStruct((t, QKV_W), BF16),
                   jax.ShapeDtypeStruct((batch, 4, seq_len // 4, QKV_W), BF16),
                   jax.ShapeDtypeStruct((batch, 16, seq_len // 16, QKV_W), BF16),
                   jax.ShapeDtypeStruct((t, GATES_W), BF16)],
        scratch_shapes=[pltpu.VMEM((tm, d), BF16),
                        pltpu.VMEM((QKV_W // V7X_LANES, tm, V7X_LANES), F32),
                        pltpu.VMEM((IN_PROJ_SORT_RING, tm, V7X_LANES), F32)],
        compiler_params=_params(("parallel",), 58),
        name="in_proj",
    )(x, g, w, *rope)
    return z1.reshape(batch, 1, seq_len, QKV_W), z4, z16, gates


def _rope_tables(seq_len):
    inv = jnp.float32(ROPE_THETA) ** (-jnp.arange(0, ROT_DIM, 2, dtype=F32) / ROT_DIM)
    ang = jnp.arange(seq_len, dtype=F32)[:, None] * inv[None, :]
    cos, sin = jnp.cos(ang), jnp.sin(ang)
    pad = HEAD_DIM - ROT_DIM
    cos_t = jnp.concatenate([cos, cos, jnp.ones((seq_len, pad), F32)], axis=1)
    sin_t = jnp.concatenate([-sin, sin, jnp.zeros((seq_len, pad), F32)], axis=1)
    return cos_t, sin_t


def _norm_matmul_kernel(x_ref, g_ref, w_ref, o_ref, h_ref):
    @pl.when(pl.program_id(1) == 0)
    def _():
        h_ref[...] = _rms(x_ref[...], g_ref[...]).astype(BF16)

    o_ref[...] = jnp.dot(h_ref[...], w_ref[...], preferred_element_type=F32).astype(BF16)


def _norm_matmul(x, g, w, *, tm, tn):
    rows, d = x.shape
    n = w.shape[1]
    assert rows % tm == 0 and n % tn == 0
    return pl.pallas_call(
        _norm_matmul_kernel,
        grid=(rows // tm, n // tn),
        in_specs=[pl.BlockSpec((tm, d), lambda i, j: (i, 0)),
                  pl.BlockSpec((1, d), lambda i, j: (0, 0)),
                  pl.BlockSpec((d, tn), lambda i, j: (0, j))],
        out_specs=pl.BlockSpec((tm, tn), lambda i, j: (i, j)),
        out_shape=jax.ShapeDtypeStruct((rows, n), BF16),
        scratch_shapes=[pltpu.VMEM((tm, d), BF16)],
        compiler_params=_params(("parallel", "arbitrary"), 40),
        name="norm_matmul",
    )(x, g, w)


def _branch_kernel(q_ref, kp_ref, k_ref, kn_ref, vp_ref, v_ref, vn_ref, o_ref, lse_ref,
                   *, res_len, dil):
    tq = q_ref.shape[0]
    r = pl.program_id(2)
    base = pl.program_id(1) * tq
    keys = jnp.concatenate([kp_ref[...], k_ref[...], kn_ref[...]], axis=0)
    vals = jnp.concatenate([vp_ref[...], v_ref[...], vn_ref[...]], axis=0)
    tb = ATT_Q_BLOCK
    nk = tb + 2 * BAND_HALF
    qi = lax.broadcasted_iota(jnp.int32, (tb, nk), 0)
    kj = lax.broadcasted_iota(jnp.int32, (tb, nk), 1)
    off = kj - qi
    band = (off >= 0) & (off <= 2 * BAND_HALF)
    lane = lax.broadcasted_iota(jnp.int32, (tb, V7X_LANES), 1)
    scale = HEAD_DIM ** -0.5
    exp2_scale = scale * 1.4426950408889634
    ones = jnp.ones((nk, HEAD_DIM), BF16)
    for blk in range(tq // tb):
        kpos = base + blk * tb - BAND_HALF + kj
        valid = band & (kpos >= 0) & (kpos < res_len)
        qrows = slice(blk * tb, (blk + 1) * tb)
        krows = slice(blk * tb, blk * tb + nk)
        if dil == 1:
            out_rows = pl.ds(blk * tb, tb)
        else:
            out_rows = pl.ds(r + dil * blk * tb, tb, stride=dil)
        m_all = jnp.zeros((tb, V7X_LANES), F32)
        l_all = jnp.ones((tb, V7X_LANES), F32)
        for p in range(HEAD_PAIRS):
            pair = []
            for h in (2 * p, 2 * p + 1):
                cols = slice(h * HEAD_DIM, (h + 1) * HEAD_DIM)
                s = lax.dot_general(q_ref[qrows, cols], keys[krows, cols],
                                    (((1,), (1,)), ((), ())),
                                    preferred_element_type=F32)
                s = jnp.where(valid, s, NEG)
                m = jnp.max(s, axis=-1, keepdims=True)
                e = jnp.exp2((s - m) * exp2_scale)
                v_ones = jnp.concatenate([vals[krows, cols], ones], axis=-1)
                ol = jnp.dot(e.astype(BF16), v_ones, preferred_element_type=F32)
                l = ol[:, HEAD_DIM:]
                pair.append(ol[:, :HEAD_DIM] / l)
                m_all = jnp.where(lane == h, m, m_all)
                l_all = jnp.where(lane == h, l, l_all)
            o_ref[p, out_rows, :] = _pack_bf16_pair(*pair)
        lse_ref[out_rows, :] = m_all * scale + jnp.log(l_all)


def _dilated_branch(z, dil):
    batch, _, res_len, _ = z.shape
    seq_len = res_len * dil
    tq = min(ATT_MAX_Q_TILE, res_len)
    assert res_len % tq == 0 and tq % ATT_Q_BLOCK == 0 and tq % BAND_HALF == 0
    halo_per_tile = tq // BAND_HALF
    n_halo_blocks = res_len // BAND_HALF

    def main(part):
        return pl.BlockSpec((None, None, tq, ATT_W), lambda b, c, r: (b, r, c, part))

    def prev(part):
        return pl.BlockSpec((None, None, BAND_HALF, ATT_W),
                            lambda b, c, r: (b, r, jnp.maximum(c * halo_per_tile - 1, 0), part))

    def nxt(part):
        return pl.BlockSpec((None, None, BAND_HALF, ATT_W),
                            lambda b, c, r: (b, r, jnp.minimum((c + 1) * halo_per_tile,
                                                               n_halo_blocks - 1), part))

    return pl.pallas_call(
        functools.partial(_branch_kernel, res_len=res_len, dil=dil),
        grid=(batch, res_len // tq, dil),
        in_specs=[main(0), prev(1), main(1), nxt(1), prev(2), main(2), nxt(2)],
        out_specs=[pl.BlockSpec((None, HEAD_PAIRS, tq * dil, V7X_LANES),
                                lambda b, c, r: (b, 0, c, 0)),
                   pl.BlockSpec((None, tq * dil, V7X_LANES), lambda b, c, r: (b, c, 0))],
        out_shape=[jax.ShapeDtypeStruct((batch, HEAD_PAIRS, seq_len, V7X_LANES), jnp.uint32),
                   jax.ShapeDtypeStruct((batch, seq_len, V7X_LANES), F32)],
        compiler_params=_params(("parallel", "parallel", "arbitrary"), 48),
        name=f"dilated_branch_d{dil}",
    )(z, z, z, z, z, z, z)


def _mixer_out_kernel(o1_ref, o2_ref, o3_ref, l1_ref, l2_ref, l3_ref,
                      gb_ref, gc_ref, gh_ref, cp_ref, hp_ref, cn_ref, hn_ref,
                      cw_ref, ga_ref, gcv_ref, wo_ref, gpost_ref, x_ref, ex_ref, out_ref,
                      *, tiles_per_seq):
    i = pl.program_id(0)
    tm = x_ref.shape[0]

    u = gc_ref[...].astype(F32) * gh_ref[...].astype(F32)
    last = CONV_HALO_ROWS - 1
    u_prev = cp_ref[last:, :].astype(F32) * hp_ref[last:, :].astype(F32)
    u_next = cn_ref[:1, :].astype(F32) * hn_ref[:1, :].astype(F32)
    pos = i % tiles_per_seq
    u_prev = jnp.where(pos == 0, jnp.zeros_like(u_prev), u_prev)
    u_next = jnp.where(pos == tiles_per_seq - 1, jnp.zeros_like(u_next), u_next)
    row = lax.broadcasted_iota(jnp.int32, u.shape, 0)
    up = jnp.where(row == 0, u_prev, pltpu.roll(u, 1, axis=0))
    dn = jnp.where(row == tm - 1, u_next, pltpu.roll(u, tm - 1, axis=0))
    cw = cw_ref[...]
    y = up * cw[0:1, :] + u * cw[1:2, :] + dn * cw[2:3, :]
    cnv = gb_ref[...].astype(F32) * y
    cnv_n = _rms(cnv, gcv_ref[...])

    l1, l2, l3 = l1_ref[...], l2_ref[...], l3_ref[...]
    mx = jnp.maximum(jnp.maximum(l1, l2), l3)
    e1, e2, e3 = jnp.exp(l1 - mx), jnp.exp(l2 - mx), jnp.exp(l3 - mx)
    den = e1 + e2 + e3

    def spread(w):
        hi = w.astype(BF16)
        lo = (w - hi.astype(F32)).astype(BF16)
        return jnp.dot(jnp.concatenate([hi, lo], axis=-1), ex_ref[...],
                       preferred_element_type=F32)

    att = None
    for e, o_ref in ((e1, o1_ref), (e2, o2_ref), (e3, o3_ref)):
        o = jnp.concatenate([_unpack_bf16_pair(o_ref[p], idx)
                             for p in range(HEAD_PAIRS) for idx in range(2)], axis=-1)
        term = spread(e / den) * o
        att = term if att is None else att + term
    att_n = _rms(att, ga_ref[...])

    mixed = jnp.concatenate([att_n.astype(BF16), cnv_n.astype(BF16)], axis=-1)
    mix = jnp.dot(mixed, wo_ref[...], preferred_element_type=F32)
    out_ref[...] = x_ref[...] + _rms(mix, gpost_ref[...])


def _mixer_out(x, gates, branches, conv_w, g_attn, g_conv, w_o, g_post, *, seq_len, tm):
    t = x.shape[0]
    assert t % tm == 0 and seq_len % tm == 0 and tm % CONV_HALO_ROWS == 0
    tiles_per_seq = seq_len // tm
    halo_per_tile = tm // CONV_HALO_ROWS
    n_halo = t // CONV_HALO_ROWS

    def rows(width):
        return pl.BlockSpec((tm, width), lambda i: (i, 0))

    packed = pl.BlockSpec((None, HEAD_PAIRS, tm, V7X_LANES),
                          lambda i: (i // tiles_per_seq, 0, i % tiles_per_seq, 0))
    lse = pl.BlockSpec((None, tm, V7X_LANES),
                       lambda i: (i // tiles_per_seq, i % tiles_per_seq, 0))

    def gate(k):
        return pl.BlockSpec((tm, CONV_W), lambda i: (i, k))

    def halo_prev(k):
        return pl.BlockSpec((CONV_HALO_ROWS, CONV_W),
                            lambda i: (jnp.maximum(i * halo_per_tile - 1, 0), k))

    def halo_next(k):
        return pl.BlockSpec((CONV_HALO_ROWS, CONV_W),
                            lambda i: (jnp.minimum((i + 1) * halo_per_tile, n_halo - 1), k))

    def vec(width, nrows=1):
        return pl.BlockSpec((nrows, width), lambda i: (0, 0))

    head_of_col = jnp.arange(ATT_W, dtype=jnp.int32) // HEAD_DIM
    lane = jnp.arange(2 * V7X_LANES, dtype=jnp.int32) % V7X_LANES
    expand = (lane[:, None] == head_of_col[None, :]).astype(BF16)

    (o1, l1), (o2, l2), (o3, l3) = branches
    return pl.pallas_call(
        functools.partial(_mixer_out_kernel, tiles_per_seq=tiles_per_seq),
        grid=(t // tm,),
        in_specs=[packed, packed, packed, lse, lse, lse,
                  gate(0), gate(1), gate(2),
                  halo_prev(1), halo_prev(2), halo_next(1), halo_next(2),
                  vec(CONV_W, 3), vec(ATT_W), vec(CONV_W),
                  _resident((D_MODEL, D_MODEL), lambda i: (0, 0)),
                  vec(D_MODEL), rows(D_MODEL),
                  _resident((2 * V7X_LANES, ATT_W), lambda i: (0, 0))],
        out_specs=rows(D_MODEL),
        out_shape=jax.ShapeDtypeStruct((t, D_MODEL), F32),
        compiler_params=_params(("parallel",), 48),
        name="mixer_out",
    )(o1, o2, o3, l1, l2, l3, gates, gates, gates, gates, gates, gates, gates,
      conv_w, g_attn, g_conv, w_o, g_post, x, expand)


def _cross_attn_kernel(x_ref, gpre_ref, wq_ref, k_ref, v_ref, wo_ref, gpost_ref, out_ref):
    x = x_ref[...]
    xn = _rms(x, gpre_ref[...]).astype(BF16)
    q = jnp.dot(xn, wq_ref[...], preferred_element_type=F32).astype(BF16)
    scale = X_HEAD_DIM ** -0.5
    heads = []
    for h in range(X_HEADS):
        cols = slice(h * X_HEAD_DIM, (h + 1) * X_HEAD_DIM)
        s = lax.dot_general(q[:, cols], k_ref[:, cols], (((1,), (1,)), ((), ())),
                            preferred_element_type=F32) * scale
        m = jnp.max(s, axis=-1, keepdims=True)
        e = jnp.exp(s - m)
        p = e / jnp.sum(e, axis=-1, keepdims=True)
        o = jnp.dot(p.astype(BF16), v_ref[:, cols], preferred_element_type=F32)
        heads.append(o.astype(BF16))
    o = jnp.concatenate(heads, axis=-1)
    xa = jnp.dot(o, wo_ref[...], preferred_element_type=F32)
    out_ref[...] = x + _rms(xa, gpost_ref[...])


def _cross_attn(x, kv, g_pre, w_q, w_o, g_post, *, seq_len, tm):
    t = x.shape[0]
    assert t % tm == 0 and seq_len % tm == 0
    tiles_per_seq = seq_len // tm
    vec = pl.BlockSpec((1, D_MODEL), lambda i: (0, 0))
    rows = pl.BlockSpec((tm, D_MODEL), lambda i: (i, 0))
    return pl.pallas_call(
        _cross_attn_kernel,
        grid=(t // tm,),
        in_specs=[rows, vec,
                  _resident((D_MODEL, D_MODEL), lambda i: (0, 0)),
                  pl.BlockSpec((N_MEM, D_MODEL), lambda i: (i // tiles_per_seq, 0)),
                  pl.BlockSpec((N_MEM, D_MODEL), lambda i: (i // tiles_per_seq, 1)),
                  _resident((D_MODEL, D_MODEL), lambda i: (0, 0)),
                  vec],
        out_specs=rows,
        out_shape=jax.ShapeDtypeStruct((t, D_MODEL), F32),
        compiler_params=_params(("parallel",), 56),
        name="cross_attn",
    )(x, g_pre, w_q, kv, kv, w_o, g_post)


def _ffn_kernel(x_ref, gpre_ref, wg_ref, wu_ref, wd_ref, gpost_ref, out_ref, h_ref):
    j = pl.program_id(1)

    @pl.when(j == 0)
    def _():
        h_ref[...] = _rms(x_ref[...], gpre_ref[...]).astype(BF16)
        out_ref[...] = jnp.zeros_like(out_ref)

    rc = FFN_ROW_CHUNK
    for ch in range(x_ref.shape[0] // rc):
        rows = slice(ch * rc, (ch + 1) * rc)
        h = h_ref[rows, :]
        gate = jnp.dot(h, wg_ref[...], preferred_element_type=F32)
        up = jnp.dot(h, wu_ref[...], preferred_element_type=F32)
        act = (gate * jax.nn.sigmoid(gate) * up).astype(BF16)
        out_ref[rows, :] += jnp.dot(act, wd_ref[...], preferred_element_type=F32)

    @pl.when(j == pl.num_programs(1) - 1)
    def _():
        out_ref[...] = x_ref[...] + _rms(out_ref[...], gpost_ref[...])


def _ffn(x, g_pre, w_gate, w_up, w_down, g_post, *, tm, tf):
    t = x.shape[0]
    assert t % tm == 0 and D_FF % tf == 0
    vec = pl.BlockSpec((1, D_MODEL), lambda i, j: (0, 0))
    rows = pl.BlockSpec((tm, D_MODEL), lambda i, j: (i, 0))
    return pl.pallas_call(
        _ffn_kernel,
        grid=(t // tm, D_FF // tf),
        in_specs=[rows, vec,
                  pl.BlockSpec((D_MODEL, tf), lambda i, j: (0, j)),
                  pl.BlockSpec((D_MODEL, tf), lambda i, j: (0, j)),
                  pl.BlockSpec((tf, D_MODEL), lambda i, j: (j, 0)),
                  vec],
        out_specs=rows,
        out_shape=jax.ShapeDtypeStruct((t, D_MODEL), F32),
        scratch_shapes=[pltpu.VMEM((tm, D_MODEL), BF16)],
        compiler_params=_params(("parallel", "arbitrary"), 58),
        name="swiglu_ffn",
    )(x, g_pre, w_gate, w_up, w_down, g_post)


def _trunk(x, mem, layers, rope):
    batch, seq_len, _ = x.shape
    x = x.reshape(batch * seq_len, D_MODEL)
    mem = mem.reshape(batch * N_MEM, D_MODEL)
    for p in layers:
        *zs, gates = _in_proj(x, p["g_mix_pre"], p["w_in"], rope, batch=batch,
                              seq_len=seq_len, tm=IN_PROJ_ROW_TILE, tn=IN_PROJ_COL_TILE)
        branches = [_dilated_branch(z, dil) for z, dil in zip(zs, DILATIONS)]
        x = _mixer_out(x, gates, branches, p["conv_w"], p["g_attn_out"], p["g_conv_out"],
                       p["w_o"], p["g_mix_post"], seq_len=seq_len, tm=ROW_TILE)
        kv = _norm_matmul(mem, p["g_mem"], p["w_xkv"], tm=min(mem.shape[0], KV_ROW_TILE),
                          tn=KV_COL_TILE)
        x = _cross_attn(x, kv, p["g_x_pre"], p["w_xq"], p["w_xo"], p["g_x_post"],
                        seq_len=seq_len, tm=ROW_TILE)
        x = _ffn(x, p["g_ffn_pre"], p["w_gate"], p["w_up"], p["w_down"], p["g_ffn_post"],
                 tm=FFN_ROW_TILE, tf=FFN_COL_TILE)
    return x.reshape(batch, seq_len, D_MODEL)


def kernel(x_prompt, x_sample, mem_prompt, mem_sample, g_mix_pre, w_in, conv_w, g_attn_out, g_conv_out, w_o, g_mix_post, g_x_pre, g_mem, w_xq, w_xk, w_xv, w_xo, g_x_post, g_ffn_pre, w_gate, w_up, w_down, g_ffn_post):
    depth = w_in.shape[0]
    layers = []
    for l in range(depth):
        layers.append({
            "g_mix_pre": g_mix_pre[l][None, :], "w_in": w_in[l].astype(BF16),
            "conv_w": conv_w[l], "g_attn_out": g_attn_out[l][None, :],
            "g_conv_out": g_conv_out[l][None, :], "w_o": w_o[l].astype(BF16),
            "g_mix_post": g_mix_post[l][None, :], "g_x_pre": g_x_pre[l][None, :],
            "g_mem": g_mem[l][None, :], "w_xq": w_xq[l].astype(BF16),
            "w_xkv": jnp.concatenate([w_xk[l], w_xv[l]], axis=1).astype(BF16),
            "w_xo": w_xo[l].astype(BF16), "g_x_post": g_x_post[l][None, :],
            "g_ffn_pre": g_ffn_pre[l][None, :], "w_gate": w_gate[l].astype(BF16),
            "w_up": w_up[l].astype(BF16), "w_down": w_down[l].astype(BF16),
            "g_ffn_post": g_ffn_post[l][None, :],
        })
    outs = []
    for x, mem in ((x_prompt, mem_prompt), (x_sample, mem_sample)):
        rope = _rope_tables(x.shape[1])
        outs.append(_trunk(x, mem, layers, rope))
    return tuple(outs)
```

```python
import functools

import jax
import jax.numpy as jnp
from jax import lax
from jax.experimental import pallas as pl
from jax.experimental.pallas import tpu as pltpu

F32 = jnp.float32
BF16 = jnp.bfloat16

D_MODEL = 2048
HEAD_DIM = 128
N_ATT_HEADS = 12
ATT_W = N_ATT_HEADS * HEAD_DIM
QKV_W = 3 * ATT_W
CONV_W = D_MODEL - ATT_W
GATES_W = 3 * CONV_W
DILATIONS = (1, 4, 16)
BAND_HALF = 64
ROT_DIM = HEAD_DIM // 4
ROPE_THETA = 500000.0
N_MEM = 256
X_HEADS = 4
X_HEAD_DIM = D_MODEL // X_HEADS
D_FF = 5632
EPS = 1e-6
NEG = -1e30

V7X_LANES = 128
V7X_BF16_SUBLANES = 16
V7X_VMEM_BYTES = 64 * 1024 * 1024
MIB = 1024 * 1024

ROW_TILE = 512
IN_PROJ_ROW_TILE = 256
IN_PROJ_COL_TILE = ATT_W
IN_PROJ_SORT_RING = 12
FFN_ROW_TILE = 1024
KV_ROW_TILE = 1024
KV_COL_TILE = 1024
FFN_COL_TILE = 512
FFN_ROW_CHUNK = 256
ATT_Q_BLOCK = 128
ATT_MAX_Q_TILE = 512
CONV_HALO_ROWS = V7X_BF16_SUBLANES
HEAD_PAIRS = N_ATT_HEADS // 2


def _params(semantics, vmem_mib):
    assert vmem_mib * MIB < V7X_VMEM_BYTES
    return pltpu.CompilerParams(dimension_semantics=semantics,
                                vmem_limit_bytes=vmem_mib * MIB)


def _resident(shape, index_map):
    return pl.BlockSpec(shape, index_map, pipeline_mode=pl.Buffered(1))


def _rms(x, g):
    ms = jnp.mean(x * x, axis=-1, keepdims=True)
    return (x * lax.rsqrt(ms + EPS)) * g


def _pack_bf16_pair(a, b):
    lo = lax.bitcast_convert_type(a.astype(BF16).astype(F32), jnp.uint32) >> 16
    hi = lax.bitcast_convert_type(b.astype(BF16).astype(F32), jnp.uint32) & jnp.uint32(0xFFFF0000)
    return lo | hi


def _unpack_bf16_pair(word, index):
    bits = (word << 16) if index == 0 else (word & jnp.uint32(0xFFFF0000))
    return lax.bitcast_convert_type(bits, F32)


def _in_proj_kernel(x_ref, g_ref, w_ref, cos_ref, sin_ref,
                    z1_ref, z4_ref, z16_ref, gates_ref, h_ref, slab_ref, slab4_ref, *, tn):
    tm = x_ref.shape[0]
    h_ref[...] = _rms(x_ref[...], g_ref[...]).astype(BF16)
    cos = cos_ref[...]
    sin = sin_ref[...]
    lane = lax.broadcasted_iota(jnp.int32, cos.shape, 1)
    first_half = lane < ROT_DIM // 2

    def project(col0):
        return jnp.dot(h_ref[...], w_ref[:, col0:col0 + tn], preferred_element_type=F32)

    for col0 in range(0, QKV_W, tn):
        acc = project(col0)
        for c in range(tn // V7X_LANES):
            lo_col = col0 + c * V7X_LANES
            cols = slice(lo_col, lo_col + V7X_LANES)
            slab = lo_col // V7X_LANES
            t = acc[:, c * V7X_LANES:(c + 1) * V7X_LANES]
            if lo_col < 2 * ATT_W:
                hi = pltpu.roll(t, HEAD_DIM - ROT_DIM // 2, axis=1)
                lo = pltpu.roll(t, ROT_DIM // 2, axis=1)
                t = t * cos + jnp.where(first_half, hi, lo) * sin
            z1_ref[:, cols] = t.astype(BF16)
            slab_ref[slab] = t
            n4, n16 = tm // 4, tm // 16
            ring = slab % slab4_ref.shape[0]
            for r in range(4):
                s4 = slab_ref[slab, pl.ds(r, n4, stride=4), :]
                z4_ref[r, :, cols] = s4.astype(BF16)
                slab4_ref[ring, r * n4:(r + 1) * n4, :] = s4
            for r in range(4):
                for t4 in range(4):
                    s16 = slab4_ref[ring, pl.ds(r * n4 + t4, n16, stride=4), :]
                    z16_ref[r + 4 * t4, :, cols] = s16.astype(BF16)
    for col0 in range(0, GATES_W, tn):
        gates_ref[:, col0:col0 + tn] = project(QKV_W + col0).astype(BF16)


def _in_proj(x, g, w, rope, *, batch, seq_len, tm, tn):
    t, d = x.shape
    n = w.shape[1]
    assert n == QKV_W + GATES_W and t % tm == 0 and seq_len % tm == 0
    assert QKV_W % tn == 0 and GATES_W % tn == 0 and tn % V7X_LANES == 0
    assert tm % (16 * V7X_BF16_SUBLANES) == 0
    tiles_per_seq = seq_len // tm

    def sorted_spec(dil):
        return pl.BlockSpec((None, dil, tm // dil, QKV_W),
                            lambda i: (i // tiles_per_seq, 0, i % tiles_per_seq, 0))

    rope_spec = pl.BlockSpec((tm, HEAD_DIM), lambda i: (i % tiles_per_seq, 0))
    z1, z4, z16, gates = pl.pallas_call(
        functools.partial(_in_proj_kernel, tn=tn),
        grid=(t // tm,),
        in_specs=[pl.BlockSpec((tm, d), lambda i: (i, 0)),
                  pl.BlockSpec((1, d), lambda i: (0, 0)),
                  _resident((d, n), lambda i: (0, 0)),
                  rope_spec, rope_spec],
        out_specs=[pl.BlockSpec((tm, QKV_W), lambda i: (i, 0)),
                   sorted_spec(4), sorted_spec(16),
                   pl.BlockSpec((tm, GATES_W), lambda i: (i, 0))],
        out_shape=[jax.ShapeDtypeStruct((t, QKV_W), BF16),
                   jax.ShapeDtypeStruct((batch, 4, seq_len // 4, QKV_W), BF16),
                   jax.ShapeDtypeStruct((batch, 16, seq_len // 16, QKV_W), BF16),
                   jax.ShapeDtypeStruct((t, GATES_W), BF16)],
        scratch_shapes=[pltpu.VMEM((tm, d), BF16),
                        pltpu.VMEM((QKV_W // V7X_LANES, tm, V7X_LANES), F32),
                        pltpu.VMEM((IN_PROJ_SORT_RING, tm, V7X_LANES), F32)],
        compiler_params=_params(("parallel",), 58),
        name="in_proj",
    )(x, g, w, *rope)
    return z1.reshape(batch, 1, seq_len, QKV_W), z4, z16, gates


def _rope_tables(seq_len):
    inv = jnp.float32(ROPE_THETA) ** (-jnp.arange(0, ROT_DIM, 2, dtype=F32) / ROT_DIM)
    ang = jnp.arange(seq_len, dtype=F32)[:, None] * inv[None, :]
    cos, sin = jnp.cos(ang), jnp.sin(ang)
    pad = HEAD_DIM - ROT_DIM
    cos_t = jnp.concatenate([cos, cos, jnp.ones((seq_len, pad), F32)], axis=1)
    sin_t = jnp.concatenate([-sin, sin, jnp.zeros((seq_len, pad), F32)], axis=1)
    return cos_t, sin_t


def _norm_matmul_kernel(x_ref, g_ref, w_ref, o_ref, h_ref):
    @pl.when(pl.program_id(1) == 0)
    def _():
        h_ref[...] = _rms(x_ref[...], g_ref[...]).astype(BF16)

    o_ref[...] = jnp.dot(h_ref[...], w_ref[...], preferred_element_type=F32).astype(BF16)


def _norm_matmul(x, g, w, *, tm, tn):
    rows, d = x.shape
    n = w.shape[1]
    assert rows % tm == 0 and n % tn == 0
    return pl.pallas_call(
        _norm_matmul_kernel,
        grid=(rows // tm, n // tn),
        in_specs=[pl.BlockSpec((tm, d), lambda i, j: (i, 0)),
                  pl.BlockSpec((1, d), lambda i, j: (0, 0)),
                  pl.BlockSpec((d, tn), lambda i, j: (0, j))],
        out_specs=pl.BlockSpec((tm, tn), lambda i, j: (i, j)),
        out_shape=jax.ShapeDtypeStruct((rows, n), BF16),
        scratch_shapes=[pltpu.VMEM((tm, d), BF16)],
        compiler_params=_params(("parallel", "arbitrary"), 40),
        name="norm_matmul",
    )(x, g, w)


def _branch_kernel(q_ref, kp_ref, k_ref, kn_ref, vp_ref, v_ref, vn_ref, o_ref, lse_ref,
                   *, res_len, dil):
    tq = q_ref.shape[0]
    r = pl.program_id(2)
    base = pl.program_id(1) * tq
    keys = jnp.concatenate([kp_ref[...], k_ref[...], kn_ref[...]], axis=0)
    vals = jnp.concatenate([vp_ref[...], v_ref[...], vn_ref[...]], axis=0)
    tb = ATT_Q_BLOCK
    nk = tb + 2 * BAND_HALF
    qi = lax.broadcasted_iota(jnp.int32, (tb, nk), 0)
    kj = lax.broadcasted_iota(jnp.int32, (tb, nk), 1)
    off = kj - qi
    band = (off >= 0) & (off <= 2 * BAND_HALF)
    lane = lax.broadcasted_iota(jnp.int32, (tb, V7X_LANES), 1)
    scale = HEAD_DIM ** -0.5
    exp2_scale = scale * 1.4426950408889634
    ones = jnp.ones((nk, HEAD_DIM), BF16)
    for blk in range(tq // tb):
        kpos = base + blk * tb - BAND_HALF + kj
        valid = band & (kpos >= 0) & (kpos < res_len)
        qrows = slice(blk * tb, (blk + 1) * tb)
        krows = slice(blk * tb, blk * tb + nk)
        if dil == 1:
            out_rows = pl.ds(blk * tb, tb)
        else:
            out_rows = pl.ds(r + dil * blk * tb, tb, stride=dil)
        m_all = jnp.zeros((tb, V7X_LANES), F32)
        l_all = jnp.ones((tb, V7X_LANES), F32)
        for p in range(HEAD_PAIRS):
            pair = []
            for h in (2 * p, 2 * p + 1):
                cols = slice(h * HEAD_DIM, (h + 1) * HEAD_DIM)
                s = lax.dot_general(q_ref[qrows, cols], keys[krows, cols],
                                    (((1,), (1,)), ((), ())),
                                    preferred_element_type=F32)
                s = jnp.where(valid, s, NEG)
                m = jnp.max(s, axis=-1, keepdims=True)
                e = jnp.exp2((s - m) * exp2_scale)
                v_ones = jnp.concatenate([vals[krows, cols], ones], axis=-1)
                ol = jnp.dot(e.astype(BF16), v_ones, preferred_element_type=F32)
                l = ol[:, HEAD_DIM:]
                pair.append(ol[:, :HEAD_DIM] / l)
                m_all = jnp.where(lane == h, m, m_all)
                l_all = jnp.where(lane == h, l, l_all)
            o_ref[p, out_rows, :] = _pack_bf16_pair(*pair)
        lse_ref[out_rows, :] = m_all * scale + jnp.log(l_all)


def _dilated_branch(z, dil):
    batch, _, res_len, _ = z.shape
    seq_len = res_len * dil
    tq = min(ATT_MAX_Q_TILE, res_len)
    assert res_len % tq == 0 and tq % ATT_Q_BLOCK == 0 and tq % BAND_HALF == 0
    halo_per_tile = tq // BAND_HALF
    n_halo_blocks = res_len // BAND_HALF

    def main(part):
        return pl.BlockSpec((None, None, tq, ATT_W), lambda b, c, r: (b, r, c, part))

    def prev(part):
        return pl.BlockSpec((None, None, BAND_HALF, ATT_W),
                            lambda b, c, r: (b, r, jnp.maximum(c * halo_per_tile - 1, 0), part))

    def nxt(part):
        return pl.BlockSpec((None, None, BAND_HALF, ATT_W),
                            lambda b, c, r: (b, r, jnp.minimum((c + 1) * halo_per_tile,
                                                               n_halo_blocks - 1), part))

    return pl.pallas_call(
        functools.partial(_branch_kernel, res_len=res_len, dil=dil),
        grid=(batch, res_len // tq, dil),
        in_specs=[main(0), prev(1), main(1), nxt(1), prev(2), main(2), nxt(2)],
        out_specs=[pl.BlockSpec((None, HEAD_PAIRS, tq * dil, V7X_LANES),
                                lambda b, c, r: (b, 0, c, 0)),
                   pl.BlockSpec((None, tq * dil, V7X_LANES), lambda b, c, r: (b, c, 0))],
        out_shape=[jax.ShapeDtypeStruct((batch, HEAD_PAIRS, seq_len, V7X_LANES), jnp.uint32),
                   jax.ShapeDtypeStruct((batch, seq_len, V7X_LANES), F32)],
        compiler_params=_params(("parallel", "parallel", "arbitrary"), 48),
        name=f"dilated_branch_d{dil}",
    )(z, z, z, z, z, z, z)


def _mixer_out_kernel(o1_ref, o2_ref, o3_ref, l1_ref, l2_ref, l3_ref,
                      gb_ref, gc_ref, gh_ref, cp_ref, hp_ref, cn_ref, hn_ref,
                      cw_ref, ga_ref, gcv_ref, wo_ref, gpost_ref, x_ref, ex_ref, out_ref,
                      *, tiles_per_seq):
    i = pl.program_id(0)
    tm = x_ref.shape[0]

    u = gc_ref[...].astype(F32) * gh_ref[...].astype(F32)
    last = CONV_HALO_ROWS - 1
    u_prev = cp_ref[last:, :].astype(F32) * hp_ref[last:, :].astype(F32)
    u_next = cn_ref[:1, :].astype(F32) * hn_ref[:1, :].astype(F32)
    pos = i % tiles_per_seq
    u_prev = jnp.where(pos == 0, jnp.zeros_like(u_prev), u_prev)
    u_next = jnp.where(pos == tiles_per_seq - 1, jnp.zeros_like(u_next), u_next)
    row = lax.broadcasted_iota(jnp.int32, u.shape, 0)
    up = jnp.where(row == 0, u_prev, pltpu.roll(u, 1, axis=0))
    dn = jnp.where(row == tm - 1, u_next, pltpu.roll(u, tm - 1, axis=0))
    cw = cw_ref[...]
    y = up * cw[0:1, :] + u * cw[1:2, :] + dn * cw[2:3, :]
    cnv = gb_ref[...].astype(F32) * y
    cnv_n = _rms(cnv, gcv_ref[...])

    l1, l2, l3 = l1_ref[...], l2_ref[...], l3_ref[...]
    mx = jnp.maximum(jnp.maximum(l1, l2), l3)
    e1, e2, e3 = jnp.exp(l1 - mx), jnp.exp(l2 - mx), jnp.exp(l3 - mx)
    den = e1 + e2 + e3

    def spread(w):
        hi = w.astype(BF16)
        lo = (w - hi.astype(F32)).astype(BF16)
        return jnp.dot(jnp.concatenate([hi, lo], axis=-1), ex_ref[...],
                       preferred_element_type=F32)

    att = None
    for e, o_ref in ((e1, o1_ref), (e2, o2_ref), (e3, o3_ref)):
        o = jnp.concatenate([_unpack_bf16_pair(o_ref[p], idx)
                             for p in range(HEAD_PAIRS) for idx in range(2)], axis=-1)
        term = spread(e / den) * o
        att = term if att is None else att + term
    att_n = _rms(att, ga_ref[...])

    mixed = jnp.concatenate([att_n.astype(BF16), cnv_n.astype(BF16)], axis=-1)
    mix = jnp.dot(mixed, wo_ref[...], preferred_element_type=F32)
    out_ref[...] = x_ref[...] + _rms(mix, gpost_ref[...])


def _mixer_out(x, gates, branches, conv_w, g_attn, g_conv, w_o, g_post, *, seq_len, tm):
    t = x.shape[0]
    assert t % tm == 0 and seq_len % tm == 0 and tm % CONV_HALO_ROWS == 0
    tiles_per_seq = seq_len // tm
    halo_per_tile = tm // CONV_HALO_ROWS
    n_halo = t // CONV_HALO_ROWS

    def rows(width):
        return pl.BlockSpec((tm, width), lambda i: (i, 0))

    packed = pl.BlockSpec((None, HEAD_PAIRS, tm, V7X_LANES),
                          lambda i: (i // tiles_per_seq, 0, i % tiles_per_seq, 0))
    lse = pl.BlockSpec((None, tm, V7X_LANES),
                       lambda i: (i // tiles_per_seq, i % tiles_per_seq, 0))

    def gate(k):
        return pl.BlockSpec((tm, CONV_W), lambda i: (i, k))

    def halo_prev(k):
        return pl.BlockSpec((CONV_HALO_ROWS, CONV_W),
                            lambda i: (jnp.maximum(i * halo_per_tile - 1, 0), k))

    def halo_next(k):
        return pl.BlockSpec((CONV_HALO_ROWS, CONV_W),
                            lambda i: (jnp.minimum((i + 1) * halo_per_tile, n_halo - 1), k))

    def vec(width, nrows=1):
        return pl.BlockSpec((nrows, width), lambda i: (0, 0))

    head_of_col = jnp.arange(ATT_W, dtype=jnp.int32) // HEAD_DIM
    lane = jnp.arange(2 * V7X_LANES, dtype=jnp.int32) % V7X_LANES
    expand = (lane[:, None] == head_of_col[None, :]).astype(BF16)

    (o1, l1), (o2, l2), (o3, l3) = branches
    return pl.pallas_call(
        functools.partial(_mixer_out_kernel, tiles_per_seq=tiles_per_seq),
        grid=(t // tm,),
        in_specs=[packed, packed, packed, lse, lse, lse,
                  gate(0), gate(1), gate(2),
                  halo_prev(1), halo_prev(2), halo_next(1), halo_next(2),
                  vec(CONV_W, 3), vec(ATT_W), vec(CONV_W),
                  _resident((D_MODEL, D_MODEL), lambda i: (0, 0)),
                  vec(D_MODEL), rows(D_MODEL),
                  _resident((2 * V7X_LANES, ATT_W), lambda i: (0, 0))],
        out_specs=rows(D_MODEL),
        out_shape=jax.ShapeDtypeStruct((t, D_MODEL), F32),
        compiler_params=_params(("parallel",), 48),
        name="mixer_out",
    )(o1, o2, o3, l1, l2, l3, gates, gates, gates, gates, gates, gates, gates,
      conv_w, g_attn, g_conv, w_o, g_post, x, expand)


def _cross_attn_kernel(x_ref, gpre_ref, wq_ref, k_ref, v_ref, wo_ref, gpost_ref, out_ref):
    x = x_ref[...]
    xn = _rms(x, gpre_ref[...]).astype(BF16)
    q = jnp.dot(xn, wq_ref[...], preferred_element_type=F32).astype(BF16)
    scale = X_HEAD_DIM ** -0.5
    heads = []
    for h in range(X_HEADS):
        cols = slice(h * X_HEAD_DIM, (h + 1) * X_HEAD_DIM)
        s = lax.dot_general(q[:, cols], k_ref[:, cols], (((1,), (1,)), ((), ())),
                            preferred_element_type=F32) * scale
        m = jnp.max(s, axis=-1, keepdims=True)
        e = jnp.exp(s - m)
        p = e / jnp.sum(e, axis=-1, keepdims=True)
        o = jnp.dot(p.astype(BF16), v_ref[:, cols], preferred_element_type=F32)
        heads.append(o.astype(BF16))
    o = jnp.concatenate(heads, axis=-1)
    xa = jnp.dot(o, wo_ref[...], preferred_element_type=F32)
    out_ref[...] = x + _rms(xa, gpost_ref[...])


def _cross_attn(x, kv, g_pre, w_q, w_o, g_post, *, seq_len, tm):
    t = x.shape[0]
    assert t % tm == 0 and seq_len % tm == 0
    tiles_per_seq = seq_len // tm
    vec = pl.BlockSpec((1, D_MODEL), lambda i: (0, 0))
    rows = pl.BlockSpec((tm, D_MODEL), lambda i: (i, 0))
    return pl.pallas_call(
        _cross_attn_kernel,
        grid=(t // tm,),
        in_specs=[rows, vec,
                  _resident((D_MODEL, D_MODEL), lambda i: (0, 0)),
                  pl.BlockSpec((N_MEM, D_MODEL), lambda i: (i // tiles_per_seq, 0)),
                  pl.BlockSpec((N_MEM, D_MODEL), lambda i: (i // tiles_per_seq, 1)),
                  _resident((D_MODEL, D_MODEL), lambda i: (0, 0)),
                  vec],
        out_specs=rows,
        out_shape=jax.ShapeDtypeStruct((t, D_MODEL), F32),
        compiler_params=_params(("parallel",), 56),
        name="cross_attn",
    )(x, g_pre, w_q, kv, kv, w_o, g_post)


def _ffn_kernel(x_ref, gpre_ref, wg_ref, wu_ref, wd_ref, gpost_ref, out_ref, h_ref):
    j = pl.program_id(1)
    last = pl.num_programs(1) - 1

    rc = FFN_ROW_CHUNK

    def body(first, final):
        for ch in range(x_ref.shape[0] // rc):
            rows = slice(ch * rc, (ch + 1) * rc)
            if first:
                h = _rms(x_ref[rows, :], gpre_ref[...]).astype(BF16)
                h_ref[rows, :] = h
            else:
                h = h_ref[rows, :]
            gate = jnp.dot(h, wg_ref[...], preferred_element_type=F32)
            up = jnp.dot(h, wu_ref[...], preferred_element_type=F32)
            act = (gate * jax.nn.sigmoid(gate) * up).astype(BF16)
            part = jnp.dot(act, wd_ref[...], preferred_element_type=F32)
            if first:
                out_ref[rows, :] = part
            elif final:
                acc = out_ref[rows, :] + part
                out_ref[rows, :] = x_ref[rows, :] + _rms(acc, gpost_ref[...])
            else:
                out_ref[rows, :] += part

    @pl.when(j == 0)
    def _():
        body(True, False)

    @pl.when((j > 0) & (j < last))
    def _():
        body(False, False)

    @pl.when(j == last)
    def _():
        body(False, True)


def _ffn(x, g_pre, w_gate, w_up, w_down, g_post, *, tm, tf):
    t = x.shape[0]
    assert t % tm == 0 and D_FF % tf == 0 and D_FF // tf >= 2 and tm % FFN_ROW_CHUNK == 0
    vec = pl.BlockSpec((1, D_MODEL), lambda i, j: (0, 0))
    rows = pl.BlockSpec((tm, D_MODEL), lambda i, j: (i, 0))
    return pl.pallas_call(
        _ffn_kernel,
        grid=(t // tm, D_FF // tf),
        in_specs=[rows, vec,
                  pl.BlockSpec((D_MODEL, tf), lambda i, j: (0, j)),
                  pl.BlockSpec((D_MODEL, tf), lambda i, j: (0, j)),
                  pl.BlockSpec((tf, D_MODEL), lambda i, j: (j, 0)),
                  vec],
        out_specs=rows,
        out_shape=jax.ShapeDtypeStruct((t, D_MODEL), F32),
        scratch_shapes=[pltpu.VMEM((tm, D_MODEL), BF16)],
        compiler_params=_params(("parallel", "arbitrary"), 58),
        name="swiglu_ffn",
    )(x, g_pre, w_gate, w_up, w_down, g_post)


def _trunk(x, mem, layers, rope):
    batch, seq_len, _ = x.shape
    x = x.reshape(batch * seq_len, D_MODEL)
    mem = mem.reshape(batch * N_MEM, D_MODEL)
    for p in layers:
        *zs, gates = _in_proj(x, p["g_mix_pre"], p["w_in"], rope, batch=batch,
                              seq_len=seq_len, tm=IN_PROJ_ROW_TILE, tn=IN_PROJ_COL_TILE)
        branches = [_dilated_branch(z, dil) for z, dil in zip(zs, DILATIONS)]
        x = _mixer_out(x, gates, branches, p["conv_w"], p["g_attn_out"], p["g_conv_out"],
                       p["w_o"], p["g_mix_post"], seq_len=seq_len, tm=ROW_TILE)
        kv = _norm_matmul(mem, p["g_mem"], p["w_xkv"], tm=min(mem.shape[0], KV_ROW_TILE),
                          tn=KV_COL_TILE)
        x = _cross_attn(x, kv, p["g_x_pre"], p["w_xq"], p["w_xo"], p["g_x_post"],
                        seq_len=seq_len, tm=ROW_TILE)
        x = _ffn(x, p["g_ffn_pre"], p["w_gate"], p["w_up"], p["w_down"], p["g_ffn_post"],
                 tm=FFN_ROW_TILE, tf=FFN_COL_TILE)
    return x.reshape(batch, seq_len, D_MODEL)


def kernel(x_prompt, x_sample, mem_prompt, mem_sample, g_mix_pre, w_in, conv_w, g_attn_out, g_conv_out, w_o, g_mix_post, g_x_pre, g_mem, w_xq, w_xk, w_xv, w_xo, g_x_post, g_ffn_pre, w_gate, w_up, w_down, g_ffn_post):
    depth = w_in.shape[0]
    layers = []
    for l in range(depth):
        layers.append({
            "g_mix_pre": g_mix_pre[l][None, :], "w_in": w_in[l].astype(BF16),
            "conv_w": conv_w[l], "g_attn_out": g_attn_out[l][None, :],
            "g_conv_out": g_conv_out[l][None, :], "w_o": w_o[l].astype(BF16),
            "g_mix_post": g_mix_post[l][None, :], "g_x_pre": g_x_pre[l][None, :],
            "g_mem": g_mem[l][None, :], "w_xq": w_xq[l].astype(BF16),
            "w_xkv": jnp.concatenate([w_xk[l], w_xv[l]], axis=1).astype(BF16),
            "w_xo": w_xo[l].astype(BF16), "g_x_post": g_x_post[l][None, :],
            "g_ffn_pre": g_ffn_pre[l][None, :], "w_gate": w_gate[l].astype(BF16),
            "w_up": w_up[l].astype(BF16), "w_down": w_down[l].astype(BF16),
            "g_ffn_post": g_ffn_post[l][None, :],
        })
    outs = []
    for x, mem in ((x_prompt, mem_prompt), (x_sample, mem_sample)):
        rope = _rope_tables(x.shape[1])
        outs.append(_trunk(x, mem, layers, rope))
    return tuple(outs)
```

```python
import functools

import jax
import jax.numpy as jnp
from jax import lax
from jax.experimental import pallas as pl
from jax.experimental.pallas import tpu as pltpu

F32 = jnp.float32
BF16 = jnp.bfloat16

D_MODEL = 2048
HEAD_DIM = 128
N_ATT_HEADS = 12
ATT_W = N_ATT_HEADS * HEAD_DIM
QKV_W = 3 * ATT_W
CONV_W = D_MODEL - ATT_W
GATES_W = 3 * CONV_W
DILATIONS = (1, 4, 16)
BAND_HALF = 64
ROT_DIM = HEAD_DIM // 4
ROPE_THETA = 500000.0
N_MEM = 256
X_HEADS = 4
X_HEAD_DIM = D_MODEL // X_HEADS
D_FF = 5632
EPS = 1e-6
NEG = -1e30
LOG2_E = 1.4426950408889634

V7X_LANES = 128
V7X_BF16_SUBLANES = 16
V7X_VMEM_BYTES = 64 * 1024 * 1024
MIB = 1024 * 1024

ROW_TILE = 512
IN_PROJ_ROW_TILE = 256
IN_PROJ_COL_TILE = ATT_W
IN_PROJ_SORT_RING = 12
FFN_ROW_TILE = 1024
KV_ROW_TILE = 1024
KV_COL_TILE = 1024
FFN_COL_TILE = 512
FFN_ROW_CHUNK = 256
ATT_Q_BLOCK = 128
ATT_MAX_Q_TILE = 512
CONV_HALO_ROWS = V7X_BF16_SUBLANES
HEAD_PAIRS = N_ATT_HEADS // 2


def _params(semantics, vmem_mib):
    assert vmem_mib * MIB < V7X_VMEM_BYTES
    return pltpu.CompilerParams(dimension_semantics=semantics,
                                vmem_limit_bytes=vmem_mib * MIB)


def _resident(shape, index_map):
    return pl.BlockSpec(shape, index_map, pipeline_mode=pl.Buffered(1))


def _layer_block(layer, block, index_map, resident=False):
    def stacked_index(*grid_idx):
        return (layer,) + tuple(index_map(*grid_idx))
    if resident:
        return _resident((None,) + tuple(block), stacked_index)
    return pl.BlockSpec((None,) + tuple(block), stacked_index)


def _rms(x, g):
    ms = jnp.mean(x * x, axis=-1, keepdims=True)
    return (x * lax.rsqrt(ms + EPS)) * g


def _pack_bf16_pair(a, b):
    lo = lax.bitcast_convert_type(a.astype(BF16).astype(F32), jnp.uint32) >> 16
    hi = lax.bitcast_convert_type(b.astype(BF16).astype(F32), jnp.uint32) & jnp.uint32(0xFFFF0000)
    return lo | hi


def _unpack_bf16_pair(word, index):
    bits = (word << 16) if index == 0 else (word & jnp.uint32(0xFFFF0000))
    return lax.bitcast_convert_type(bits, F32)


def _in_proj_kernel(x_ref, g_ref, w_ref, cos_ref, sin_ref,
                    z1_ref, z4_ref, z16_ref, gates_ref, h_ref, slab_ref, slab4_ref, *, tn):
    tm = x_ref.shape[0]
    h_ref[...] = _rms(x_ref[...], g_ref[...]).astype(BF16)
    cos = cos_ref[...]
    sin = sin_ref[...]
    lane = lax.broadcasted_iota(jnp.int32, cos.shape, 1)
    first_half = lane < ROT_DIM // 2

    def project(col0):
        return jnp.dot(h_ref[...], w_ref[:, col0:col0 + tn], preferred_element_type=F32)

    for col0 in range(0, QKV_W, tn):
        acc = project(col0)
        for c in range(tn // V7X_LANES):
            lo_col = col0 + c * V7X_LANES
            cols = slice(lo_col, lo_col + V7X_LANES)
            slab = lo_col // V7X_LANES
            t = acc[:, c * V7X_LANES:(c + 1) * V7X_LANES]
            if lo_col < 2 * ATT_W:
                hi = pltpu.roll(t, HEAD_DIM - ROT_DIM // 2, axis=1)
                lo = pltpu.roll(t, ROT_DIM // 2, axis=1)
                t = t * cos + jnp.where(first_half, hi, lo) * sin
            z1_ref[:, cols] = t.astype(BF16)
            slab_ref[slab] = t
            n4, n16 = tm // 4, tm // 16
            ring = slab % slab4_ref.shape[0]
            for r in range(4):
                s4 = slab_ref[slab, pl.ds(r, n4, stride=4), :]
                z4_ref[r, :, cols] = s4.astype(BF16)
                slab4_ref[ring, r * n4:(r + 1) * n4, :] = s4
            for r in range(4):
                for t4 in range(4):
                    s16 = slab4_ref[ring, pl.ds(r * n4 + t4, n16, stride=4), :]
                    z16_ref[r + 4 * t4, :, cols] = s16.astype(BF16)
    for col0 in range(0, GATES_W, tn):
        gates_ref[:, col0:col0 + tn] = project(QKV_W + col0).astype(BF16)


def _in_proj(x, g, w, rope, *, layer, batch, seq_len, tm, tn):
    t, d = x.shape
    n = w.shape[-1]
    assert n == QKV_W + GATES_W and t % tm == 0 and seq_len % tm == 0
    assert QKV_W % tn == 0 and GATES_W % tn == 0 and tn % V7X_LANES == 0
    assert tm % (16 * V7X_BF16_SUBLANES) == 0
    tiles_per_seq = seq_len // tm

    def sorted_spec(dil):
        return pl.BlockSpec((None, dil, tm // dil, QKV_W),
                            lambda i: (i // tiles_per_seq, 0, i % tiles_per_seq, 0))

    rope_spec = pl.BlockSpec((tm, HEAD_DIM), lambda i: (i % tiles_per_seq, 0))
    z1, z4, z16, gates = pl.pallas_call(
        functools.partial(_in_proj_kernel, tn=tn),
        grid=(t // tm,),
        in_specs=[pl.BlockSpec((tm, d), lambda i: (i, 0)),
                  _layer_block(layer, (1, d), lambda i: (0, 0)),
                  _layer_block(layer, (d, n), lambda i: (0, 0), resident=True),
                  rope_spec, rope_spec],
        out_specs=[pl.BlockSpec((tm, QKV_W), lambda i: (i, 0)),
                   sorted_spec(4), sorted_spec(16),
                   pl.BlockSpec((tm, GATES_W), lambda i: (i, 0))],
        out_shape=[jax.ShapeDtypeStruct((t, QKV_W), BF16),
                   jax.ShapeDtypeStruct((batch, 4, seq_len // 4, QKV_W), BF16),
                   jax.ShapeDtypeStruct((batch, 16, seq_len // 16, QKV_W), BF16),
                   jax.ShapeDtypeStruct((t, GATES_W), BF16)],
        scratch_shapes=[pltpu.VMEM((tm, d), BF16),
                        pltpu.VMEM((QKV_W // V7X_LANES, tm, V7X_LANES), F32),
                        pltpu.VMEM((IN_PROJ_SORT_RING, tm, V7X_LANES), F32)],
        compiler_params=_params(("parallel",), 58),
        name="in_proj",
    )(x, g, w, *rope)
    return z1.reshape(batch, 1, seq_len, QKV_W), z4, z16, gates


def _rope_tables(seq_len):
    inv = jnp.float32(ROPE_THETA) ** (-jnp.arange(0, ROT_DIM, 2, dtype=F32) / ROT_DIM)
    ang = jnp.arange(seq_len, dtype=F32)[:, None] * inv[None, :]
    cos, sin = jnp.cos(ang), jnp.sin(ang)
    pad = HEAD_DIM - ROT_DIM
    cos_t = jnp.concatenate([cos, cos, jnp.ones((seq_len, pad), F32)], axis=1)
    sin_t = jnp.concatenate([-sin, sin, jnp.zeros((seq_len, pad), F32)], axis=1)
    return cos_t, sin_t


def _norm_matmul_kernel(x_ref, g_ref, w_ref, o_ref, h_ref):
    @pl.when(pl.program_id(1) == 0)
    def _():
        h_ref[...] = _rms(x_ref[...], g_ref[...]).astype(BF16)

    o_ref[...] = jnp.dot(h_ref[...], w_ref[...], preferred_element_type=F32).astype(BF16)


def _norm_matmul(x, g, w, *, layer, tm, tn):
    rows, d = x.shape
    n = w.shape[-1]
    assert rows % tm == 0 and n % tn == 0
    return pl.pallas_call(
        _norm_matmul_kernel,
        grid=(rows // tm, n // tn),
        in_specs=[pl.BlockSpec((tm, d), lambda i, j: (i, 0)),
                  _layer_block(layer, (1, d), lambda i, j: (0, 0)),
                  _layer_block(layer, (d, tn), lambda i, j: (0, j))],
        out_specs=pl.BlockSpec((tm, tn), lambda i, j: (i, j)),
        out_shape=jax.ShapeDtypeStruct((rows, n), BF16),
        scratch_shapes=[pltpu.VMEM((tm, d), BF16)],
        compiler_params=_params(("parallel", "arbitrary"), 40),
        name="norm_matmul",
    )(x, g, w)


def _branch_kernel(q_ref, kp_ref, k_ref, kn_ref, vp_ref, v_ref, vn_ref, o_ref, lse_ref,
                   *, res_len, dil):
    rs, tq, _ = q_ref.shape
    base = pl.program_id(1) * tq
    tb = ATT_Q_BLOCK
    nk = tb + 2 * BAND_HALF
    qi = lax.broadcasted_iota(jnp.int32, (tb, nk), 0)
    kj = lax.broadcasted_iota(jnp.int32, (tb, nk), 1)
    off = kj - qi
    band = (off >= 0) & (off <= 2 * BAND_HALF)
    lane = lax.broadcasted_iota(jnp.int32, (tb, V7X_LANES), 1)
    scale = HEAD_DIM ** -0.5
    exp2_scale = scale * LOG2_E
    ones = jnp.ones((nk, HEAD_DIM), BF16)
    for rr in range(rs):
        r = pl.program_id(2) * rs + rr
        keys = jnp.concatenate([kp_ref[rr], k_ref[rr], kn_ref[rr]], axis=0)
        vals = jnp.concatenate([vp_ref[rr], v_ref[rr], vn_ref[rr]], axis=0)
        for blk in range(tq // tb):
            kpos = base + blk * tb - BAND_HALF + kj
            valid = band & (kpos >= 0) & (kpos < res_len)
            qrows = slice(blk * tb, (blk + 1) * tb)
            krows = slice(blk * tb, blk * tb + nk)
            if dil == 1:
                out_rows = pl.ds(blk * tb, tb)
            else:
                out_rows = pl.ds(r + dil * blk * tb, tb, stride=dil)
            m_all = jnp.zeros((tb, V7X_LANES), F32)
            l_all = jnp.ones((tb, V7X_LANES), F32)
            for p in range(HEAD_PAIRS):
                pair = []
                for h in (2 * p, 2 * p + 1):
                    cols = slice(h * HEAD_DIM, (h + 1) * HEAD_DIM)
                    s = lax.dot_general(q_ref[rr, qrows, cols], keys[krows, cols],
                                        (((1,), (1,)), ((), ())),
                                        preferred_element_type=F32)
                    s = jnp.where(valid, s, NEG)
                    m = jnp.max(s, axis=-1, keepdims=True)
                    e = jnp.exp2((s - m) * exp2_scale)
                    v_ones = jnp.concatenate([vals[krows, cols], ones], axis=-1)
                    ol = jnp.dot(e.astype(BF16), v_ones, preferred_element_type=F32)
                    l = ol[:, HEAD_DIM:]
                    pair.append(ol[:, :HEAD_DIM] / l)
                    m_all = jnp.where(lane == h, m, m_all)
                    l_all = jnp.where(lane == h, l, l_all)
                o_ref[p, out_rows, :] = _pack_bf16_pair(*pair)
            lse_ref[out_rows, :] = m_all * scale + jnp.log(l_all)


def _dilated_branch(z, dil):
    batch, _, res_len, _ = z.shape
    seq_len = res_len * dil
    tq = min(ATT_MAX_Q_TILE, res_len)
    rs = min(dil, ATT_MAX_Q_TILE // tq)
    assert res_len % tq == 0 and tq % ATT_Q_BLOCK == 0 and tq % BAND_HALF == 0
    assert dil % rs == 0
    halo_per_tile = tq // BAND_HALF
    n_halo_blocks = res_len // BAND_HALF

    def main(part):
        return pl.BlockSpec((None, rs, tq, ATT_W), lambda b, c, g: (b, g, c, part))

    def prev(part):
        return pl.BlockSpec((None, rs, BAND_HALF, ATT_W),
                            lambda b, c, g: (b, g, jnp.maximum(c * halo_per_tile - 1, 0), part))

    def nxt(part):
        return pl.BlockSpec((None, rs, BAND_HALF, ATT_W),
                            lambda b, c, g: (b, g, jnp.minimum((c + 1) * halo_per_tile,
                                                               n_halo_blocks - 1), part))

    return pl.pallas_call(
        functools.partial(_branch_kernel, res_len=res_len, dil=dil),
        grid=(batch, res_len // tq, dil // rs),
        in_specs=[main(0), prev(1), main(1), nxt(1), prev(2), main(2), nxt(2)],
        out_specs=[pl.BlockSpec((None, HEAD_PAIRS, tq * dil, V7X_LANES),
                                lambda b, c, g: (b, 0, c, 0)),
                   pl.BlockSpec((None, tq * dil, V7X_LANES), lambda b, c, g: (b, c, 0))],
        out_shape=[jax.ShapeDtypeStruct((batch, HEAD_PAIRS, seq_len, V7X_LANES), jnp.uint32),
                   jax.ShapeDtypeStruct((batch, seq_len, V7X_LANES), F32)],
        compiler_params=_params(("parallel", "parallel", "arbitrary"), 48),
        name=f"dilated_branch_d{dil}",
    )(z, z, z, z, z, z, z)


def _mixer_out_kernel(o1_ref, o2_ref, o3_ref, l1_ref, l2_ref, l3_ref,
                      gb_ref, gc_ref, gh_ref, cp_ref, hp_ref, cn_ref, hn_ref,
                      cw_ref, ga_ref, gcv_ref, wo_ref, gpost_ref, x_ref, ex_ref, out_ref,
                      *, tiles_per_seq):
    i = pl.program_id(0)
    tm = x_ref.shape[0]

    u = gc_ref[...].astype(F32) * gh_ref[...].astype(F32)
    last = CONV_HALO_ROWS - 1
    u_prev = cp_ref[last:, :].astype(F32) * hp_ref[last:, :].astype(F32)
    u_next = cn_ref[:1, :].astype(F32) * hn_ref[:1, :].astype(F32)
    pos = i % tiles_per_seq
    u_prev = jnp.where(pos == 0, jnp.zeros_like(u_prev), u_prev)
    u_next = jnp.where(pos == tiles_per_seq - 1, jnp.zeros_like(u_next), u_next)
    row = lax.broadcasted_iota(jnp.int32, u.shape, 0)
    up = jnp.where(row == 0, u_prev, pltpu.roll(u, 1, axis=0))
    dn = jnp.where(row == tm - 1, u_next, pltpu.roll(u, tm - 1, axis=0))
    cw = cw_ref[...]
    y = up * cw[0:1, :] + u * cw[1:2, :] + dn * cw[2:3, :]
    cnv = gb_ref[...].astype(F32) * y
    cnv_n = _rms(cnv, gcv_ref[...])

    l1, l2, l3 = l1_ref[...], l2_ref[...], l3_ref[...]
    mx = jnp.maximum(jnp.maximum(l1, l2), l3)
    e1, e2, e3 = jnp.exp(l1 - mx), jnp.exp(l2 - mx), jnp.exp(l3 - mx)
    den = e1 + e2 + e3

    def spread(w):
        hi = w.astype(BF16)
        lo = (w - hi.astype(F32)).astype(BF16)
        return jnp.dot(jnp.concatenate([hi, lo], axis=-1), ex_ref[...],
                       preferred_element_type=F32)

    att = None
    for e, o_ref in ((e1, o1_ref), (e2, o2_ref), (e3, o3_ref)):
        o = jnp.concatenate([_unpack_bf16_pair(o_ref[p], idx)
                             for p in range(HEAD_PAIRS) for idx in range(2)], axis=-1)
        term = spread(e / den) * o
        att = term if att is None else att + term
    att_n = _rms(att, ga_ref[...])

    mixed = jnp.concatenate([att_n.astype(BF16), cnv_n.astype(BF16)], axis=-1)
    mix = jnp.dot(mixed, wo_ref[...], preferred_element_type=F32)
    out_ref[...] = x_ref[...] + _rms(mix, gpost_ref[...])


def _mixer_out(x, gates, branches, conv_w, g_attn, g_conv, w_o, g_post, *, layer, seq_len, tm):
    t = x.shape[0]
    assert t % tm == 0 and seq_len % tm == 0 and tm % CONV_HALO_ROWS == 0
    tiles_per_seq = seq_len // tm
    halo_per_tile = tm // CONV_HALO_ROWS
    n_halo = t // CONV_HALO_ROWS

    def rows(width):
        return pl.BlockSpec((tm, width), lambda i: (i, 0))

    packed = pl.BlockSpec((None, HEAD_PAIRS, tm, V7X_LANES),
                          lambda i: (i // tiles_per_seq, 0, i % tiles_per_seq, 0))
    lse = pl.BlockSpec((None, tm, V7X_LANES),
                       lambda i: (i // tiles_per_seq, i % tiles_per_seq, 0))

    def gate(k):
        return pl.BlockSpec((tm, CONV_W), lambda i: (i, k))

    def halo_prev(k):
        return pl.BlockSpec((CONV_HALO_ROWS, CONV_W),
                            lambda i: (jnp.maximum(i * halo_per_tile - 1, 0), k))

    def halo_next(k):
        return pl.BlockSpec((CONV_HALO_ROWS, CONV_W),
                            lambda i: (jnp.minimum((i + 1) * halo_per_tile, n_halo - 1), k))

    def vec(width, nrows=1):
        return _layer_block(layer, (nrows, width), lambda i: (0, 0))

    head_of_col = jnp.arange(ATT_W, dtype=jnp.int32) // HEAD_DIM
    lane = jnp.arange(2 * V7X_LANES, dtype=jnp.int32) % V7X_LANES
    expand = (lane[:, None] == head_of_col[None, :]).astype(BF16)

    (o1, l1), (o2, l2), (o3, l3) = branches
    return pl.pallas_call(
        functools.partial(_mixer_out_kernel, tiles_per_seq=tiles_per_seq),
        grid=(t // tm,),
        in_specs=[packed, packed, packed, lse, lse, lse,
                  gate(0), gate(1), gate(2),
                  halo_prev(1), halo_prev(2), halo_next(1), halo_next(2),
                  vec(CONV_W, 3), vec(ATT_W), vec(CONV_W),
                  _layer_block(layer, (D_MODEL, D_MODEL), lambda i: (0, 0), resident=True),
                  vec(D_MODEL), rows(D_MODEL),
                  _resident((2 * V7X_LANES, ATT_W), lambda i: (0, 0))],
        out_specs=rows(D_MODEL),
        out_shape=jax.ShapeDtypeStruct((t, D_MODEL), F32),
        compiler_params=_params(("parallel",), 48),
        name="mixer_out",
    )(o1, o2, o3, l1, l2, l3, gates, gates, gates, gates, gates, gates, gates,
      conv_w, g_attn, g_conv, w_o, g_post, x, expand)


def _cross_attn_kernel(x_ref, gpre_ref, wq_ref, k_ref, v_ref, wo_ref, gpost_ref, out_ref):
    x = x_ref[...]
    xn = _rms(x, gpre_ref[...]).astype(BF16)
    q = jnp.dot(xn, wq_ref[...], preferred_element_type=F32).astype(BF16)
    scale = X_HEAD_DIM ** -0.5
    heads = []
    for h in range(X_HEADS):
        cols = slice(h * X_HEAD_DIM, (h + 1) * X_HEAD_DIM)
        s = lax.dot_general(q[:, cols], k_ref[:, cols], (((1,), (1,)), ((), ())),
                            preferred_element_type=F32) * scale
        m = jnp.max(s, axis=-1, keepdims=True)
        e = jnp.exp(s - m)
        p = e / jnp.sum(e, axis=-1, keepdims=True)
        o = jnp.dot(p.astype(BF16), v_ref[:, cols], preferred_element_type=F32)
        heads.append(o.astype(BF16))
    o = jnp.concatenate(heads, axis=-1)
    xa = jnp.dot(o, wo_ref[...], preferred_element_type=F32)
    out_ref[...] = x + _rms(xa, gpost_ref[...])


def _cross_attn(x, kv, g_pre, w_q, w_o, g_post, *, layer, seq_len, tm):
    t = x.shape[0]
    assert t % tm == 0 and seq_len % tm == 0
    tiles_per_seq = seq_len // tm
    vec = _layer_block(layer, (1, D_MODEL), lambda i: (0, 0))
    rows = pl.BlockSpec((tm, D_MODEL), lambda i: (i, 0))
    return pl.pallas_call(
        _cross_attn_kernel,
        grid=(t // tm,),
        in_specs=[rows, vec,
                  _layer_block(layer, (D_MODEL, D_MODEL), lambda i: (0, 0), resident=True),
                  pl.BlockSpec((N_MEM, D_MODEL), lambda i: (i // tiles_per_seq, 0)),
                  pl.BlockSpec((N_MEM, D_MODEL), lambda i: (i // tiles_per_seq, 1)),
                  _layer_block(layer, (D_MODEL, D_MODEL), lambda i: (0, 0), resident=True),
                  vec],
        out_specs=rows,
        out_shape=jax.ShapeDtypeStruct((t, D_MODEL), F32),
        compiler_params=_params(("parallel",), 56),
        name="cross_attn",
    )(x, g_pre, w_q, kv, kv, w_o, g_post)


def _ffn_kernel(x_ref, gpre_ref, wg_ref, wu_ref, wd_ref, gpost_ref, out_ref, h_ref):
    j = pl.program_id(1)
    last = pl.num_programs(1) - 1

    rc = FFN_ROW_CHUNK

    def body(first, final):
        for ch in range(x_ref.shape[0] // rc):
            rows = slice(ch * rc, (ch + 1) * rc)
            if first:
                h = _rms(x_ref[rows, :], gpre_ref[...]).astype(BF16)
                h_ref[rows, :] = h
            else:
                h = h_ref[rows, :]
            gate = jnp.dot(h, wg_ref[...], preferred_element_type=F32)
            up = jnp.dot(h, wu_ref[...], preferred_element_type=F32)
            act = (gate * jax.nn.sigmoid(gate) * up).astype(BF16)
            part = jnp.dot(act, wd_ref[...], preferred_element_type=F32)
            if first:
                out_ref[rows, :] = part
            elif final:
                acc = out_ref[rows, :] + part
                out_ref[rows, :] = x_ref[rows, :] + _rms(acc, gpost_ref[...])
            else:
                out_ref[rows, :] += part

    @pl.when(j == 0)
    def _():
        body(True, False)

    @pl.when((j > 0) & (j < last))
    def _():
        body(False, False)

    @pl.when(j == last)
    def _():
        body(False, True)


def _ffn(x, g_pre, w_gate, w_up, w_down, g_post, *, layer, tm, tf):
    t = x.shape[0]
    assert t % tm == 0 and D_FF % tf == 0 and D_FF // tf >= 2 and tm % FFN_ROW_CHUNK == 0
    vec = _layer_block(layer, (1, D_MODEL), lambda i, j: (0, 0))
    rows = pl.BlockSpec((tm, D_MODEL), lambda i, j: (i, 0))
    return pl.pallas_call(
        _ffn_kernel,
        grid=(t // tm, D_FF // tf),
        in_specs=[rows, vec,
                  _layer_block(layer, (D_MODEL, tf), lambda i, j: (0, j)),
                  _layer_block(layer, (D_MODEL, tf), lambda i, j: (0, j)),
                  _layer_block(layer, (tf, D_MODEL), lambda i, j: (j, 0)),
                  vec],
        out_specs=rows,
        out_shape=jax.ShapeDtypeStruct((t, D_MODEL), F32),
        scratch_shapes=[pltpu.VMEM((tm, D_MODEL), BF16)],
        compiler_params=_params(("parallel", "arbitrary"), 58),
        name="swiglu_ffn",
    )(x, g_pre, w_gate, w_up, w_down, g_post)


def _trunk(x, mem, params, rope):
    batch, seq_len, _ = x.shape
    x = x.reshape(batch * seq_len, D_MODEL)
    mem = mem.reshape(batch * N_MEM, D_MODEL)
    p = params
    for layer in range(p["w_in"].shape[0]):
        *zs, gates = _in_proj(x, p["g_mix_pre"], p["w_in"], rope, layer=layer, batch=batch,
                              seq_len=seq_len, tm=IN_PROJ_ROW_TILE, tn=IN_PROJ_COL_TILE)
        branches = [_dilated_branch(z, dil) for z, dil in zip(zs, DILATIONS)]
        x = _mixer_out(x, gates, branches, p["conv_w"], p["g_attn_out"], p["g_conv_out"],
                       p["w_o"], p["g_mix_post"], layer=layer, seq_len=seq_len, tm=ROW_TILE)
        kv = _norm_matmul(mem, p["g_mem"], p["w_xkv"], layer=layer,
                          tm=min(mem.shape[0], KV_ROW_TILE), tn=KV_COL_TILE)
        x = _cross_attn(x, kv, p["g_x_pre"], p["w_xq"], p["w_xo"], p["g_x_post"],
                        layer=layer, seq_len=seq_len, tm=ROW_TILE)
        x = _ffn(x, p["g_ffn_pre"], p["w_gate"], p["w_up"], p["w_down"], p["g_ffn_post"],
                 layer=layer, tm=FFN_ROW_TILE, tf=FFN_COL_TILE)
    return x.reshape(batch, seq_len, D_MODEL)


def kernel(x_prompt, x_sample, mem_prompt, mem_sample, g_mix_pre, w_in, conv_w, g_attn_out, g_conv_out, w_o, g_mix_post, g_x_pre, g_mem, w_xq, w_xk, w_xv, w_xo, g_x_post, g_ffn_pre, w_gate, w_up, w_down, g_ffn_post):
    gains = {"g_mix_pre": g_mix_pre, "g_attn_out": g_attn_out, "g_conv_out": g_conv_out,
             "g_mix_post": g_mix_post, "g_x_pre": g_x_pre, "g_mem": g_mem,
             "g_x_post": g_x_post, "g_ffn_pre": g_ffn_pre, "g_ffn_post": g_ffn_post}
    params = {name: g[:, None, :] for name, g in gains.items()}
    params.update({
        "conv_w": conv_w,
        "w_in": w_in.astype(BF16), "w_o": w_o.astype(BF16), "w_xq": w_xq.astype(BF16),
        "w_xkv": jnp.concatenate([w_xk, w_xv], axis=2).astype(BF16),
        "w_xo": w_xo.astype(BF16), "w_gate": w_gate.astype(BF16),
        "w_up": w_up.astype(BF16), "w_down": w_down.astype(BF16),
    })
    outs = []
    for x, mem in ((x_prompt, mem_prompt), (x_sample, mem_sample)):
        rope = _rope_tables(x.shape[1])
        outs.append(_trunk(x, mem, params, rope))
    return tuple(outs)
```

```python
import functools

import jax
import jax.numpy as jnp
from jax import lax
from jax.experimental import pallas as pl
from jax.experimental.pallas import tpu as pltpu

F32 = jnp.float32
BF16 = jnp.bfloat16

D_MODEL = 2048
HEAD_DIM = 128
N_ATT_HEADS = 12
ATT_W = N_ATT_HEADS * HEAD_DIM
QKV_W = 3 * ATT_W
CONV_W = D_MODEL - ATT_W
GATES_W = 3 * CONV_W
DILATIONS = (1, 4, 16)
BAND_HALF = 64
ROT_DIM = HEAD_DIM // 4
ROPE_THETA = 500000.0
N_MEM = 256
X_HEADS = 4
X_HEAD_DIM = D_MODEL // X_HEADS
D_FF = 5632
EPS = 1e-6
NEG = -1e30
LOG2_E = 1.4426950408889634

V7X_LANES = 128
V7X_BF16_SUBLANES = 16
V7X_VMEM_BYTES = 64 * 1024 * 1024
MIB = 1024 * 1024

ROW_TILE = 512
IN_PROJ_ROW_TILE = 256
IN_PROJ_COL_TILE = ATT_W
IN_PROJ_SORT_RING = 12
FFN_ROW_TILE = 1024
KV_ROW_TILE = 1024
KV_COL_TILE = 1024
FFN_COL_TILE = 512
FFN_ROW_CHUNK = 512
ATT_Q_BLOCK = 128
ATT_MAX_Q_TILE = 512
CONV_HALO_ROWS = V7X_BF16_SUBLANES
HEAD_PAIRS = N_ATT_HEADS // 2


def _params(semantics, vmem_mib):
    assert vmem_mib * MIB < V7X_VMEM_BYTES
    return pltpu.CompilerParams(dimension_semantics=semantics,
                                vmem_limit_bytes=vmem_mib * MIB)


def _resident(shape, index_map):
    return pl.BlockSpec(shape, index_map, pipeline_mode=pl.Buffered(1))


def _layer_block(layer, block, index_map, resident=False):
    def stacked_index(*grid_idx):
        return (layer,) + tuple(index_map(*grid_idx))
    if resident:
        return _resident((None,) + tuple(block), stacked_index)
    return pl.BlockSpec((None,) + tuple(block), stacked_index)


def _rms(x, g):
    ms = jnp.mean(x * x, axis=-1, keepdims=True)
    return (x * lax.rsqrt(ms + EPS)) * g


def _pack_bf16_pair(a, b):
    lo = lax.bitcast_convert_type(a.astype(BF16).astype(F32), jnp.uint32) >> 16
    hi = lax.bitcast_convert_type(b.astype(BF16).astype(F32), jnp.uint32) & jnp.uint32(0xFFFF0000)
    return lo | hi


def _unpack_bf16_pair(word, index):
    bits = (word << 16) if index == 0 else (word & jnp.uint32(0xFFFF0000))
    return lax.bitcast_convert_type(bits, F32)


def _in_proj_kernel(x_ref, g_ref, w_ref, cos_ref, sin_ref,
                    z1_ref, z4_ref, z16_ref, gates_ref, h_ref, slab_ref, slab4_ref, *, tn):
    tm = x_ref.shape[0]
    h_ref[...] = _rms(x_ref[...], g_ref[...]).astype(BF16)
    cos = cos_ref[...]
    sin = sin_ref[...]
    lane = lax.broadcasted_iota(jnp.int32, cos.shape, 1)
    first_half = lane < ROT_DIM // 2

    def project(col0):
        return jnp.dot(h_ref[...], w_ref[:, col0:col0 + tn], preferred_element_type=F32)

    for col0 in range(0, QKV_W, tn):
        acc = project(col0)
        for c in range(tn // V7X_LANES):
            lo_col = col0 + c * V7X_LANES
            cols = slice(lo_col, lo_col + V7X_LANES)
            slab = lo_col // V7X_LANES
            t = acc[:, c * V7X_LANES:(c + 1) * V7X_LANES]
            if lo_col < 2 * ATT_W:
                hi = pltpu.roll(t, HEAD_DIM - ROT_DIM // 2, axis=1)
                lo = pltpu.roll(t, ROT_DIM // 2, axis=1)
                t = t * cos + jnp.where(first_half, hi, lo) * sin
            z1_ref[:, cols] = t.astype(BF16)
            slab_ref[slab] = t
            n4, n16 = tm // 4, tm // 16
            ring = slab % slab4_ref.shape[0]
            for r in range(4):
                s4 = slab_ref[slab, pl.ds(r, n4, stride=4), :]
                z4_ref[r, :, cols] = s4.astype(BF16)
                slab4_ref[ring, r * n4:(r + 1) * n4, :] = s4
            for r in range(4):
                for t4 in range(4):
                    s16 = slab4_ref[ring, pl.ds(r * n4 + t4, n16, stride=4), :]
                    z16_ref[r + 4 * t4, :, cols] = s16.astype(BF16)
    for col0 in range(0, GATES_W, tn):
        gates_ref[:, col0:col0 + tn] = project(QKV_W + col0).astype(BF16)


def _in_proj(x, g, w, rope, *, layer, batch, seq_len, tm, tn):
    t, d = x.shape
    n = w.shape[-1]
    assert n == QKV_W + GATES_W and t % tm == 0 and seq_len % tm == 0
    assert QKV_W % tn == 0 and GATES_W % tn == 0 and tn % V7X_LANES == 0
    assert tm % (16 * V7X_BF16_SUBLANES) == 0
    tiles_per_seq = seq_len // tm

    def sorted_spec(dil):
        return pl.BlockSpec((None, dil, tm // dil, QKV_W),
                            lambda i: (i // tiles_per_seq, 0, i % tiles_per_seq, 0))

    rope_spec = pl.BlockSpec((tm, HEAD_DIM), lambda i: (i % tiles_per_seq, 0))
    z1, z4, z16, gates = pl.pallas_call(
        functools.partial(_in_proj_kernel, tn=tn),
        grid=(t // tm,),
        in_specs=[pl.BlockSpec((tm, d), lambda i: (i, 0)),
                  _layer_block(layer, (1, d), lambda i: (0, 0)),
                  _layer_block(layer, (d, n), lambda i: (0, 0), resident=True),
                  rope_spec, rope_spec],
        out_specs=[pl.BlockSpec((tm, QKV_W), lambda i: (i, 0)),
                   sorted_spec(4), sorted_spec(16),
                   pl.BlockSpec((tm, GATES_W), lambda i: (i, 0))],
        out_shape=[jax.ShapeDtypeStruct((t, QKV_W), BF16),
                   jax.ShapeDtypeStruct((batch, 4, seq_len // 4, QKV_W), BF16),
                   jax.ShapeDtypeStruct((batch, 16, seq_len // 16, QKV_W), BF16),
                   jax.ShapeDtypeStruct((t, GATES_W), BF16)],
        scratch_shapes=[pltpu.VMEM((tm, d), BF16),
                        pltpu.VMEM((QKV_W // V7X_LANES, tm, V7X_LANES), F32),
                        pltpu.VMEM((IN_PROJ_SORT_RING, tm, V7X_LANES), F32)],
        compiler_params=_params(("parallel",), 58),
        name="in_proj",
    )(x, g, w, *rope)
    return z1.reshape(batch, 1, seq_len, QKV_W), z4, z16, gates


def _rope_tables(seq_len):
    inv = jnp.float32(ROPE_THETA) ** (-jnp.arange(0, ROT_DIM, 2, dtype=F32) / ROT_DIM)
    ang = jnp.arange(seq_len, dtype=F32)[:, None] * inv[None, :]
    cos, sin = jnp.cos(ang), jnp.sin(ang)
    pad = HEAD_DIM - ROT_DIM
    cos_t = jnp.concatenate([cos, cos, jnp.ones((seq_len, pad), F32)], axis=1)
    sin_t = jnp.concatenate([-sin, sin, jnp.zeros((seq_len, pad), F32)], axis=1)
    return cos_t, sin_t


def _norm_matmul_kernel(x_ref, g_ref, w_ref, o_ref, h_ref):
    @pl.when(pl.program_id(1) == 0)
    def _():
        h_ref[...] = _rms(x_ref[...], g_ref[...]).astype(BF16)

    o_ref[...] = jnp.dot(h_ref[...], w_ref[...], preferred_element_type=F32).astype(BF16)


def _norm_matmul(x, g, w, *, layer, tm, tn):
    rows, d = x.shape
    n = w.shape[-1]
    assert rows % tm == 0 and n % tn == 0
    return pl.pallas_call(
        _norm_matmul_kernel,
        grid=(rows // tm, n // tn),
        in_specs=[pl.BlockSpec((tm, d), lambda i, j: (i, 0)),
                  _layer_block(layer, (1, d), lambda i, j: (0, 0)),
                  _layer_block(layer, (d, tn), lambda i, j: (0, j))],
        out_specs=pl.BlockSpec((tm, tn), lambda i, j: (i, j)),
        out_shape=jax.ShapeDtypeStruct((rows, n), BF16),
        scratch_shapes=[pltpu.VMEM((tm, d), BF16)],
        compiler_params=_params(("parallel", "arbitrary"), 40),
        name="norm_matmul",
    )(x, g, w)


def _branch_kernel(q_ref, kp_ref, k_ref, kn_ref, vp_ref, v_ref, vn_ref, o_ref, lse_ref,
                   *, res_len, dil):
    rs, tq, _ = q_ref.shape
    base = pl.program_id(1) * tq
    tb = ATT_Q_BLOCK
    nk = tb + 2 * BAND_HALF
    qi = lax.broadcasted_iota(jnp.int32, (tb, nk), 0)
    kj = lax.broadcasted_iota(jnp.int32, (tb, nk), 1)
    off = kj - qi
    band = (off >= 0) & (off <= 2 * BAND_HALF)
    lane = lax.broadcasted_iota(jnp.int32, (tb, V7X_LANES), 1)
    scale = HEAD_DIM ** -0.5
    exp2_scale = scale * LOG2_E
    ones = jnp.ones((nk, HEAD_DIM), BF16)
    for rr in range(rs):
        r = pl.program_id(2) * rs + rr
        keys = jnp.concatenate([kp_ref[rr], k_ref[rr], kn_ref[rr]], axis=0)
        vals = jnp.concatenate([vp_ref[rr], v_ref[rr], vn_ref[rr]], axis=0)
        for blk in range(tq // tb):
            kpos = base + blk * tb - BAND_HALF + kj
            valid = band & (kpos >= 0) & (kpos < res_len)
            qrows = slice(blk * tb, (blk + 1) * tb)
            krows = slice(blk * tb, blk * tb + nk)
            if dil == 1:
                out_rows = pl.ds(blk * tb, tb)
            else:
                out_rows = pl.ds(r + dil * blk * tb, tb, stride=dil)
            m_all = jnp.zeros((tb, V7X_LANES), F32)
            l_all = jnp.ones((tb, V7X_LANES), F32)
            for p in range(HEAD_PAIRS):
                pair = []
                for h in (2 * p, 2 * p + 1):
                    cols = slice(h * HEAD_DIM, (h + 1) * HEAD_DIM)
                    s = lax.dot_general(q_ref[rr, qrows, cols], keys[krows, cols],
                                        (((1,), (1,)), ((), ())),
                                        preferred_element_type=F32)
                    s = jnp.where(valid, s, NEG)
                    m = jnp.max(s, axis=-1, keepdims=True)
                    e = jnp.exp2((s - m) * exp2_scale)
                    v_ones = jnp.concatenate([vals[krows, cols], ones], axis=-1)
                    ol = jnp.dot(e.astype(BF16), v_ones, preferred_element_type=F32)
                    l = ol[:, HEAD_DIM:]
                    pair.append(ol[:, :HEAD_DIM] / l)
                    m_all = jnp.where(lane == h, m, m_all)
                    l_all = jnp.where(lane == h, l, l_all)
                o_ref[p, out_rows, :] = _pack_bf16_pair(*pair)
            lse_ref[out_rows, :] = m_all * scale + jnp.log(l_all)


def _dilated_branch(z, dil):
    batch, _, res_len, _ = z.shape
    seq_len = res_len * dil
    tq = min(ATT_MAX_Q_TILE, res_len)
    rs = min(dil, ATT_MAX_Q_TILE // tq)
    assert res_len % tq == 0 and tq % ATT_Q_BLOCK == 0 and tq % BAND_HALF == 0
    assert dil % rs == 0
    halo_per_tile = tq // BAND_HALF
    n_halo_blocks = res_len // BAND_HALF

    def main(part):
        return pl.BlockSpec((None, rs, tq, ATT_W), lambda b, c, g: (b, g, c, part))

    def prev(part):
        return pl.BlockSpec((None, rs, BAND_HALF, ATT_W),
                            lambda b, c, g: (b, g, jnp.maximum(c * halo_per_tile - 1, 0), part))

    def nxt(part):
        return pl.BlockSpec((None, rs, BAND_HALF, ATT_W),
                            lambda b, c, g: (b, g, jnp.minimum((c + 1) * halo_per_tile,
                                                               n_halo_blocks - 1), part))

    return pl.pallas_call(
        functools.partial(_branch_kernel, res_len=res_len, dil=dil),
        grid=(batch, res_len // tq, dil // rs),
        in_specs=[main(0), prev(1), main(1), nxt(1), prev(2), main(2), nxt(2)],
        out_specs=[pl.BlockSpec((None, HEAD_PAIRS, tq * dil, V7X_LANES),
                                lambda b, c, g: (b, 0, c, 0)),
                   pl.BlockSpec((None, tq * dil, V7X_LANES), lambda b, c, g: (b, c, 0))],
        out_shape=[jax.ShapeDtypeStruct((batch, HEAD_PAIRS, seq_len, V7X_LANES), jnp.uint32),
                   jax.ShapeDtypeStruct((batch, seq_len, V7X_LANES), F32)],
        compiler_params=_params(("parallel", "parallel", "arbitrary"), 48),
        name=f"dilated_branch_d{dil}",
    )(z, z, z, z, z, z, z)


def _mixer_out_kernel(o1_ref, o2_ref, o3_ref, l1_ref, l2_ref, l3_ref,
                      gb_ref, gc_ref, gh_ref, cp_ref, hp_ref, cn_ref, hn_ref,
                      cw_ref, ga_ref, gcv_ref, wo_ref, gpost_ref, x_ref, ex_ref, out_ref,
                      *, tiles_per_seq):
    i = pl.program_id(0)
    tm = x_ref.shape[0]

    u = gc_ref[...].astype(F32) * gh_ref[...].astype(F32)
    last = CONV_HALO_ROWS - 1
    u_prev = cp_ref[last:, :].astype(F32) * hp_ref[last:, :].astype(F32)
    u_next = cn_ref[:1, :].astype(F32) * hn_ref[:1, :].astype(F32)
    pos = i % tiles_per_seq
    u_prev = jnp.where(pos == 0, jnp.zeros_like(u_prev), u_prev)
    u_next = jnp.where(pos == tiles_per_seq - 1, jnp.zeros_like(u_next), u_next)
    row = lax.broadcasted_iota(jnp.int32, u.shape, 0)
    up = jnp.where(row == 0, u_prev, pltpu.roll(u, 1, axis=0))
    dn = jnp.where(row == tm - 1, u_next, pltpu.roll(u, tm - 1, axis=0))
    cw = cw_ref[...]
    y = up * cw[0:1, :] + u * cw[1:2, :] + dn * cw[2:3, :]
    cnv = gb_ref[...].astype(F32) * y
    cnv_n = _rms(cnv, gcv_ref[...])

    l1, l2, l3 = l1_ref[...], l2_ref[...], l3_ref[...]
    mx = jnp.maximum(jnp.maximum(l1, l2), l3)
    e1, e2, e3 = jnp.exp(l1 - mx), jnp.exp(l2 - mx), jnp.exp(l3 - mx)
    den = e1 + e2 + e3

    def spread(w):
        hi = w.astype(BF16)
        lo = (w - hi.astype(F32)).astype(BF16)
        return jnp.dot(jnp.concatenate([hi, lo], axis=-1), ex_ref[...],
                       preferred_element_type=F32)

    def unpacked(o_ref):
        return jnp.concatenate([_unpack_bf16_pair(o_ref[p], idx)
                                for p in range(HEAD_PAIRS) for idx in range(2)], axis=-1)

    o3 = unpacked(o3_ref)
    att = (o3 + spread(e1 / den) * (unpacked(o1_ref) - o3)
           + spread(e2 / den) * (unpacked(o2_ref) - o3))
    att_n = _rms(att, ga_ref[...])

    mixed = jnp.concatenate([att_n.astype(BF16), cnv_n.astype(BF16)], axis=-1)
    mix = jnp.dot(mixed, wo_ref[...], preferred_element_type=F32)
    out_ref[...] = x_ref[...] + _rms(mix, gpost_ref[...])


def _mixer_out(x, gates, branches, conv_w, g_attn, g_conv, w_o, g_post, *, layer, seq_len, tm):
    t = x.shape[0]
    assert t % tm == 0 and seq_len % tm == 0 and tm % CONV_HALO_ROWS == 0
    tiles_per_seq = seq_len // tm
    halo_per_tile = tm // CONV_HALO_ROWS
    n_halo = t // CONV_HALO_ROWS

    def rows(width):
        return pl.BlockSpec((tm, width), lambda i: (i, 0))

    packed = pl.BlockSpec((None, HEAD_PAIRS, tm, V7X_LANES),
                          lambda i: (i // tiles_per_seq, 0, i % tiles_per_seq, 0))
    lse = pl.BlockSpec((None, tm, V7X_LANES),
                       lambda i: (i // tiles_per_seq, i % tiles_per_seq, 0))

    def gate(k):
        return pl.BlockSpec((tm, CONV_W), lambda i: (i, k))

    def halo_prev(k):
        return pl.BlockSpec((CONV_HALO_ROWS, CONV_W),
                            lambda i: (jnp.maximum(i * halo_per_tile - 1, 0), k))

    def halo_next(k):
        return pl.BlockSpec((CONV_HALO_ROWS, CONV_W),
                            lambda i: (jnp.minimum((i + 1) * halo_per_tile, n_halo - 1), k))

    def vec(width, nrows=1):
        return _layer_block(layer, (nrows, width), lambda i: (0, 0))

    head_of_col = jnp.arange(ATT_W, dtype=jnp.int32) // HEAD_DIM
    lane = jnp.arange(2 * V7X_LANES, dtype=jnp.int32) % V7X_LANES
    expand = (lane[:, None] == head_of_col[None, :]).astype(BF16)

    (o1, l1), (o2, l2), (o3, l3) = branches
    return pl.pallas_call(
        functools.partial(_mixer_out_kernel, tiles_per_seq=tiles_per_seq),
        grid=(t // tm,),
        in_specs=[packed, packed, packed, lse, lse, lse,
                  gate(0), gate(1), gate(2),
                  halo_prev(1), halo_prev(2), halo_next(1), halo_next(2),
                  vec(CONV_W, 3), vec(ATT_W), vec(CONV_W),
                  _layer_block(layer, (D_MODEL, D_MODEL), lambda i: (0, 0), resident=True),
                  vec(D_MODEL), rows(D_MODEL),
                  _resident((2 * V7X_LANES, ATT_W), lambda i: (0, 0))],
        out_specs=rows(D_MODEL),
        out_shape=jax.ShapeDtypeStruct((t, D_MODEL), F32),
        compiler_params=_params(("parallel",), 48),
        name="mixer_out",
    )(o1, o2, o3, l1, l2, l3, gates, gates, gates, gates, gates, gates, gates,
      conv_w, g_attn, g_conv, w_o, g_post, x, expand)


def _cross_attn_kernel(x_ref, gpre_ref, wq_ref, k_ref, v_ref, wo_ref, gpost_ref, out_ref):
    x = x_ref[...]
    xn = _rms(x, gpre_ref[...]).astype(BF16)
    q = jnp.dot(xn, wq_ref[...], preferred_element_type=F32).astype(BF16)
    scale = X_HEAD_DIM ** -0.5
    heads = []
    for h in range(X_HEADS):
        cols = slice(h * X_HEAD_DIM, (h + 1) * X_HEAD_DIM)
        s = lax.dot_general(q[:, cols], k_ref[:, cols], (((1,), (1,)), ((), ())),
                            preferred_element_type=F32) * scale
        m = jnp.max(s, axis=-1, keepdims=True)
        e = jnp.exp(s - m)
        p = e / jnp.sum(e, axis=-1, keepdims=True)
        o = jnp.dot(p.astype(BF16), v_ref[:, cols], preferred_element_type=F32)
        heads.append(o.astype(BF16))
    o = jnp.concatenate(heads, axis=-1)
    xa = jnp.dot(o, wo_ref[...], preferred_element_type=F32)
    out_ref[...] = x + _rms(xa, gpost_ref[...])


def _cross_attn(x, kv, g_pre, w_q, w_o, g_post, *, layer, seq_len, tm):
    t = x.shape[0]
    assert t % tm == 0 and seq_len % tm == 0
    tiles_per_seq = seq_len // tm
    vec = _layer_block(layer, (1, D_MODEL), lambda i: (0, 0))
    rows = pl.BlockSpec((tm, D_MODEL), lambda i: (i, 0))
    return pl.pallas_call(
        _cross_attn_kernel,
        grid=(t // tm,),
        in_specs=[rows, vec,
                  _layer_block(layer, (D_MODEL, D_MODEL), lambda i: (0, 0), resident=True),
                  pl.BlockSpec((N_MEM, D_MODEL), lambda i: (i // tiles_per_seq, 0)),
                  pl.BlockSpec((N_MEM, D_MODEL), lambda i: (i // tiles_per_seq, 1)),
                  _layer_block(layer, (D_MODEL, D_MODEL), lambda i: (0, 0), resident=True),
                  vec],
        out_specs=rows,
        out_shape=jax.ShapeDtypeStruct((t, D_MODEL), F32),
        compiler_params=_params(("parallel",), 56),
        name="cross_attn",
    )(x, g_pre, w_q, kv, kv, w_o, g_post)


def _ffn_kernel(x_ref, gpre_ref, wg_ref, wu_ref, wd_ref, gpost_ref, out_ref, h_ref):
    j = pl.program_id(1)
    last = pl.num_programs(1) - 1

    rc = FFN_ROW_CHUNK

    def body(first, final):
        for ch in range(x_ref.shape[0] // rc):
            rows = slice(ch * rc, (ch + 1) * rc)
            if first:
                h = _rms(x_ref[rows, :], gpre_ref[...]).astype(BF16)
                h_ref[rows, :] = h
            else:
                h = h_ref[rows, :]
            gate = jnp.dot(h, wg_ref[...], preferred_element_type=F32)
            up = jnp.dot(h, wu_ref[...], preferred_element_type=F32)
            act = (gate * jax.nn.sigmoid(gate) * up).astype(BF16)
            part = jnp.dot(act, wd_ref[...], preferred_element_type=F32)
            if first:
                out_ref[rows, :] = part
            elif final:
                acc = out_ref[rows, :] + part
                out_ref[rows, :] = x_ref[rows, :] + _rms(acc, gpost_ref[...])
            else:
                out_ref[rows, :] += part

    @pl.when(j == 0)
    def _():
        body(True, False)

    @pl.when((j > 0) & (j < last))
    def _():
        body(False, False)

    @pl.when(j == last)
    def _():
        body(False, True)


def _ffn(x, g_pre, w_gate, w_up, w_down, g_post, *, layer, tm, tf):
    t = x.shape[0]
    assert t % tm == 0 and D_FF % tf == 0 and D_FF // tf >= 2 and tm % FFN_ROW_CHUNK == 0
    vec = _layer_block(layer, (1, D_MODEL), lambda i, j: (0, 0))
    rows = pl.BlockSpec((tm, D_MODEL), lambda i, j: (i, 0))
    return pl.pallas_call(
        _ffn_kernel,
        grid=(t // tm, D_FF // tf),
        in_specs=[rows, vec,
                  _layer_block(layer, (None, D_MODEL, tf), lambda i, j: (j, 0, 0)),
                  _layer_block(layer, (None, D_MODEL, tf), lambda i, j: (j, 0, 0)),
                  _layer_block(layer, (tf, D_MODEL), lambda i, j: (j, 0)),
                  vec],
        out_specs=rows,
        out_shape=jax.ShapeDtypeStruct((t, D_MODEL), F32),
        scratch_shapes=[pltpu.VMEM((tm, D_MODEL), BF16)],
        compiler_params=_params(("parallel", "arbitrary"), 58),
        name="swiglu_ffn",
    )(x, g_pre, w_gate, w_up, w_down, g_post)


def _column_tiles(w, tn):
    depth, k, n = w.shape
    return w.reshape(depth, k, n // tn, tn).transpose(0, 2, 1, 3)


def _trunk(x, mem, params, rope):
    batch, seq_len, _ = x.shape
    x = x.reshape(batch * seq_len, D_MODEL)
    mem = mem.reshape(batch * N_MEM, D_MODEL)
    p = params
    for layer in range(p["w_in"].shape[0]):
        *zs, gates = _in_proj(x, p["g_mix_pre"], p["w_in"], rope, layer=layer, batch=batch,
                              seq_len=seq_len, tm=IN_PROJ_ROW_TILE, tn=IN_PROJ_COL_TILE)
        branches = [_dilated_branch(z, dil) for z, dil in zip(zs, DILATIONS)]
        x = _mixer_out(x, gates, branches, p["conv_w"], p["g_attn_out"], p["g_conv_out"],
                       p["w_o"], p["g_mix_post"], layer=layer, seq_len=seq_len, tm=ROW_TILE)
        kv = _norm_matmul(mem, p["g_mem"], p["w_xkv"], layer=layer,
                          tm=min(mem.shape[0], KV_ROW_TILE), tn=KV_COL_TILE)
        x = _cross_attn(x, kv, p["g_x_pre"], p["w_xq"], p["w_xo"], p["g_x_post"],
                        layer=layer, seq_len=seq_len, tm=ROW_TILE)
        x = _ffn(x, p["g_ffn_pre"], p["w_gate"], p["w_up"], p["w_down"], p["g_ffn_post"],
                 layer=layer, tm=FFN_ROW_TILE, tf=FFN_COL_TILE)
    return x.reshape(batch, seq_len, D_MODEL)


def kernel(x_prompt, x_sample, mem_prompt, mem_sample, g_mix_pre, w_in, conv_w, g_attn_out, g_conv_out, w_o, g_mix_post, g_x_pre, g_mem, w_xq, w_xk, w_xv, w_xo, g_x_post, g_ffn_pre, w_gate, w_up, w_down, g_ffn_post):
    gains = {"g_mix_pre": g_mix_pre, "g_attn_out": g_attn_out, "g_conv_out": g_conv_out,
             "g_mix_post": g_mix_post, "g_x_pre": g_x_pre, "g_mem": g_mem,
             "g_x_post": g_x_post, "g_ffn_pre": g_ffn_pre, "g_ffn_post": g_ffn_post}
    params = {name: g[:, None, :] for name, g in gains.items()}
    params.update({
        "conv_w": conv_w,
        "w_in": w_in.astype(BF16), "w_o": w_o.astype(BF16), "w_xq": w_xq.astype(BF16),
        "w_xkv": jnp.concatenate([w_xk, w_xv], axis=2).astype(BF16),
        "w_xo": w_xo.astype(BF16), "w_gate": _column_tiles(w_gate.astype(BF16), FFN_COL_TILE),
        "w_up": _column_tiles(w_up.astype(BF16), FFN_COL_TILE), "w_down": w_down.astype(BF16),
    })
    outs = []
    for x, mem in ((x_prompt, mem_prompt), (x_sample, mem_sample)):
        rope = _rope_tables(x.shape[1])
        outs.append(_trunk(x, mem, params, rope))
    return tuple(outs)
```

```python
import functools

import jax
import jax.numpy as jnp
from jax import lax
from jax.experimental import pallas as pl
from jax.experimental.pallas import tpu as pltpu

F32 = jnp.float32
BF16 = jnp.bfloat16

D_MODEL = 2048
HEAD_DIM = 128
N_ATT_HEADS = 12
ATT_W = N_ATT_HEADS * HEAD_DIM
QKV_W = 3 * ATT_W
CONV_W = D_MODEL - ATT_W
GATES_W = 3 * CONV_W
DILATIONS = (1, 4, 16)
BAND_HALF = 64
ROT_DIM = HEAD_DIM // 4
ROPE_THETA = 500000.0
N_MEM = 256
X_HEADS = 4
X_HEAD_DIM = D_MODEL // X_HEADS
D_FF = 5632
EPS = 1e-6
NEG = -1e30
LOG2_E = 1.4426950408889634

V7X_LANES = 128
V7X_BF16_SUBLANES = 16
V7X_VMEM_BYTES = 64 * 1024 * 1024
MIB = 1024 * 1024

ROW_TILE = 512
IN_PROJ_ROW_TILE = 256
IN_PROJ_COL_TILE = ATT_W
IN_PROJ_SORT_RING = 12
FFN_ROW_TILE = 1024
KV_ROW_TILE = 1024
KV_COL_TILE = 1024
FFN_COL_TILE = 512
FFN_ROW_CHUNK = 512
ATT_Q_BLOCK = 128
ATT_MAX_Q_TILE = 512
CONV_HALO_ROWS = V7X_BF16_SUBLANES
HEAD_PAIRS = N_ATT_HEADS // 2


def _params(semantics, vmem_mib):
    assert vmem_mib * MIB < V7X_VMEM_BYTES
    return pltpu.CompilerParams(dimension_semantics=semantics,
                                vmem_limit_bytes=vmem_mib * MIB)


def _resident(shape, index_map):
    return pl.BlockSpec(shape, index_map, pipeline_mode=pl.Buffered(1))


def _layer_block(layer, block, index_map, resident=False):
    def stacked_index(*grid_idx):
        return (layer,) + tuple(index_map(*grid_idx))
    if resident:
        return _resident((None,) + tuple(block), stacked_index)
    return pl.BlockSpec((None,) + tuple(block), stacked_index)


def _rms(x, g):
    ms = jnp.mean(x * x, axis=-1, keepdims=True)
    return (x * lax.rsqrt(ms + EPS)) * g


def _pack_bf16_pair(a, b):
    lo = lax.bitcast_convert_type(a.astype(BF16).astype(F32), jnp.uint32) >> 16
    hi = lax.bitcast_convert_type(b.astype(BF16).astype(F32), jnp.uint32) & jnp.uint32(0xFFFF0000)
    return lo | hi


def _unpack_bf16_pair(word, index):
    bits = (word << 16) if index == 0 else (word & jnp.uint32(0xFFFF0000))
    return lax.bitcast_convert_type(bits, F32)


def _in_proj_kernel(x_ref, g_ref, w_ref, cos_ref, sin_ref, *rest, tn):
    z1_refs, z4_refs, z16_refs = rest[0:3], rest[3:6], rest[6:9]
    gates_ref, h_ref, slab_ref, slab4_ref = rest[9:]
    tm = x_ref.shape[0]
    h_ref[...] = _rms(x_ref[...], g_ref[...]).astype(BF16)
    cos = cos_ref[...]
    sin = sin_ref[...]
    lane = lax.broadcasted_iota(jnp.int32, cos.shape, 1)
    first_half = lane < ROT_DIM // 2

    def project(col0):
        return jnp.dot(h_ref[...], w_ref[:, col0:col0 + tn], preferred_element_type=F32)

    for col0 in range(0, QKV_W, tn):
        acc = project(col0)
        for c in range(tn // V7X_LANES):
            lo_col = col0 + c * V7X_LANES
            part, part_col = divmod(lo_col, ATT_W)
            z1_ref, z4_ref, z16_ref = z1_refs[part], z4_refs[part], z16_refs[part]
            cols = slice(part_col, part_col + V7X_LANES)
            slab = lo_col // V7X_LANES
            t = acc[:, c * V7X_LANES:(c + 1) * V7X_LANES]
            if lo_col < 2 * ATT_W:
                hi = pltpu.roll(t, HEAD_DIM - ROT_DIM // 2, axis=1)
                lo = pltpu.roll(t, ROT_DIM // 2, axis=1)
                t = t * cos + jnp.where(first_half, hi, lo) * sin
            z1_ref[:, cols] = t.astype(BF16)
            slab_ref[slab] = t
            n4, n16 = tm // 4, tm // 16
            ring = slab % slab4_ref.shape[0]
            for r in range(4):
                s4 = slab_ref[slab, pl.ds(r, n4, stride=4), :]
                z4_ref[r, :, cols] = s4.astype(BF16)
                slab4_ref[ring, r * n4:(r + 1) * n4, :] = s4
            for r in range(4):
                for t4 in range(4):
                    s16 = slab4_ref[ring, pl.ds(r * n4 + t4, n16, stride=4), :]
                    z16_ref[r + 4 * t4, :, cols] = s16.astype(BF16)
    for col0 in range(0, GATES_W, tn):
        gates_ref[:, col0:col0 + tn] = project(QKV_W + col0).astype(BF16)


def _in_proj(x, g, w, rope, *, layer, batch, seq_len, tm, tn):
    t, d = x.shape
    n = w.shape[-1]
    assert n == QKV_W + GATES_W and t % tm == 0 and seq_len % tm == 0
    assert QKV_W % tn == 0 and GATES_W % tn == 0 and tn % V7X_LANES == 0
    assert tm % (16 * V7X_BF16_SUBLANES) == 0
    tiles_per_seq = seq_len // tm

    def sorted_spec(dil):
        return pl.BlockSpec((None, dil, tm // dil, ATT_W),
                            lambda i: (i // tiles_per_seq, 0, i % tiles_per_seq, 0))

    def sorted_shape(dil):
        return jax.ShapeDtypeStruct((batch, dil, seq_len // dil, ATT_W), BF16)

    rope_spec = pl.BlockSpec((tm, HEAD_DIM), lambda i: (i % tiles_per_seq, 0))
    *zs, gates = pl.pallas_call(
        functools.partial(_in_proj_kernel, tn=tn),
        grid=(t // tm,),
        in_specs=[pl.BlockSpec((tm, d), lambda i: (i, 0)),
                  _layer_block(layer, (1, d), lambda i: (0, 0)),
                  _layer_block(layer, (d, n), lambda i: (0, 0), resident=True),
                  rope_spec, rope_spec],
        out_specs=([pl.BlockSpec((tm, ATT_W), lambda i: (i, 0))] * 3
                   + [sorted_spec(4)] * 3 + [sorted_spec(16)] * 3
                   + [pl.BlockSpec((tm, GATES_W), lambda i: (i, 0))]),
        out_shape=([jax.ShapeDtypeStruct((t, ATT_W), BF16)] * 3
                   + [sorted_shape(4)] * 3 + [sorted_shape(16)] * 3
                   + [jax.ShapeDtypeStruct((t, GATES_W), BF16)]),
        scratch_shapes=[pltpu.VMEM((tm, d), BF16),
                        pltpu.VMEM((QKV_W // V7X_LANES, tm, V7X_LANES), F32),
                        pltpu.VMEM((IN_PROJ_SORT_RING, tm, V7X_LANES), F32)],
        compiler_params=_params(("parallel",), 58),
        name="in_proj",
    )(x, g, w, *rope)
    natural = tuple(z.reshape(batch, 1, seq_len, ATT_W) for z in zs[0:3])
    return natural, tuple(zs[3:6]), tuple(zs[6:9]), gates


def _rope_tables(seq_len):
    inv = jnp.float32(ROPE_THETA) ** (-jnp.arange(0, ROT_DIM, 2, dtype=F32) / ROT_DIM)
    ang = jnp.arange(seq_len, dtype=F32)[:, None] * inv[None, :]
    cos, sin = jnp.cos(ang), jnp.sin(ang)
    pad = HEAD_DIM - ROT_DIM
    cos_t = jnp.concatenate([cos, cos, jnp.ones((seq_len, pad), F32)], axis=1)
    sin_t = jnp.concatenate([-sin, sin, jnp.zeros((seq_len, pad), F32)], axis=1)
    return cos_t, sin_t


def _norm_matmul_kernel(x_ref, g_ref, w_ref, o_ref, h_ref):
    @pl.when(pl.program_id(1) == 0)
    def _():
        h_ref[...] = _rms(x_ref[...], g_ref[...]).astype(BF16)

    o_ref[...] = jnp.dot(h_ref[...], w_ref[...], preferred_element_type=F32).astype(BF16)


def _norm_matmul(x, g, w, *, layer, tm, tn):
    rows, d = x.shape
    n = w.shape[-1]
    assert rows % tm == 0 and n % tn == 0
    return pl.pallas_call(
        _norm_matmul_kernel,
        grid=(rows // tm, n // tn),
        in_specs=[pl.BlockSpec((tm, d), lambda i, j: (i, 0)),
                  _layer_block(layer, (1, d), lambda i, j: (0, 0)),
                  _layer_block(layer, (d, tn), lambda i, j: (0, j))],
        out_specs=pl.BlockSpec((tm, tn), lambda i, j: (i, j)),
        out_shape=jax.ShapeDtypeStruct((rows, n), BF16),
        scratch_shapes=[pltpu.VMEM((tm, d), BF16)],
        compiler_params=_params(("parallel", "arbitrary"), 40),
        name="norm_matmul",
    )(x, g, w)


def _branch_kernel(q_ref, kp_ref, k_ref, kn_ref, vp_ref, v_ref, vn_ref, o_ref, lse_ref,
                   *, res_len, dil):
    rs, tq, _ = q_ref.shape
    base = pl.program_id(1) * tq
    tb = ATT_Q_BLOCK
    nk = tb + 2 * BAND_HALF
    qi = lax.broadcasted_iota(jnp.int32, (tb, nk), 0)
    kj = lax.broadcasted_iota(jnp.int32, (tb, nk), 1)
    off = kj - qi
    band = (off >= 0) & (off <= 2 * BAND_HALF)
    lane = lax.broadcasted_iota(jnp.int32, (tb, V7X_LANES), 1)
    scale = HEAD_DIM ** -0.5
    exp2_scale = scale * LOG2_E
    ones = jnp.ones((nk, HEAD_DIM), BF16)
    for rr in range(rs):
        r = pl.program_id(2) * rs + rr
        keys = jnp.concatenate([kp_ref[rr], k_ref[rr], kn_ref[rr]], axis=0)
        vals = jnp.concatenate([vp_ref[rr], v_ref[rr], vn_ref[rr]], axis=0)
        for blk in range(tq // tb):
            kpos = base + blk * tb - BAND_HALF + kj
            valid = band & (kpos >= 0) & (kpos < res_len)
            qrows = slice(blk * tb, (blk + 1) * tb)
            krows = slice(blk * tb, blk * tb + nk)
            if dil == 1:
                out_rows = pl.ds(blk * tb, tb)
            else:
                out_rows = pl.ds(r + dil * blk * tb, tb, stride=dil)
            m_all = jnp.zeros((tb, V7X_LANES), F32)
            l_all = jnp.ones((tb, V7X_LANES), F32)
            for p in range(HEAD_PAIRS):
                pair = []
                for h in (2 * p, 2 * p + 1):
                    cols = slice(h * HEAD_DIM, (h + 1) * HEAD_DIM)
                    s = lax.dot_general(q_ref[rr, qrows, cols], keys[krows, cols],
                                        (((1,), (1,)), ((), ())),
                                        preferred_element_type=F32)
                    s = jnp.where(valid, s, NEG)
                    m = jnp.max(s, axis=-1, keepdims=True)
                    e = jnp.exp2((s - m) * exp2_scale)
                    v_ones = jnp.concatenate([vals[krows, cols], ones], axis=-1)
                    ol = jnp.dot(e.astype(BF16), v_ones, preferred_element_type=F32)
                    l = ol[:, HEAD_DIM:]
                    pair.append(ol[:, :HEAD_DIM] / l)
                    m_all = jnp.where(lane == h, m, m_all)
                    l_all = jnp.where(lane == h, l, l_all)
                o_ref[p, out_rows, :] = _pack_bf16_pair(*pair)
            lse_ref[out_rows, :] = m_all * scale + jnp.log(l_all)


def _dilated_branch(qkv, dil):
    q, k, v = qkv
    batch, _, res_len, _ = q.shape
    seq_len = res_len * dil
    tq = min(ATT_MAX_Q_TILE, res_len)
    rs = min(dil, ATT_MAX_Q_TILE // tq)
    assert res_len % tq == 0 and tq % ATT_Q_BLOCK == 0 and tq % BAND_HALF == 0
    assert dil % rs == 0
    halo_per_tile = tq // BAND_HALF
    n_halo_blocks = res_len // BAND_HALF

    main = pl.BlockSpec((None, rs, tq, ATT_W), lambda b, c, g: (b, g, c, 0))
    prev = pl.BlockSpec((None, rs, BAND_HALF, ATT_W),
                        lambda b, c, g: (b, g, jnp.maximum(c * halo_per_tile - 1, 0), 0))
    nxt = pl.BlockSpec((None, rs, BAND_HALF, ATT_W),
                       lambda b, c, g: (b, g, jnp.minimum((c + 1) * halo_per_tile,
                                                          n_halo_blocks - 1), 0))

    return pl.pallas_call(
        functools.partial(_branch_kernel, res_len=res_len, dil=dil),
        grid=(batch, res_len // tq, dil // rs),
        in_specs=[main, prev, main, nxt, prev, main, nxt],
        out_specs=[pl.BlockSpec((None, HEAD_PAIRS, tq * dil, V7X_LANES),
                                lambda b, c, g: (b, 0, c, 0)),
                   pl.BlockSpec((None, tq * dil, V7X_LANES), lambda b, c, g: (b, c, 0))],
        out_shape=[jax.ShapeDtypeStruct((batch, HEAD_PAIRS, seq_len, V7X_LANES), jnp.uint32),
                   jax.ShapeDtypeStruct((batch, seq_len, V7X_LANES), F32)],
        compiler_params=_params(("parallel", "parallel", "arbitrary"), 48),
        name=f"dilated_branch_d{dil}",
    )(q, k, k, k, v, v, v)


def _mixer_out_kernel(o1_ref, o2_ref, o3_ref, l1_ref, l2_ref, l3_ref,
                      gates_ref, gprev_ref, gnext_ref,
                      cw_ref, ga_ref, gcv_ref, wo_ref, gpost_ref, x_ref, ex_ref, out_ref,
                      *, tiles_per_seq):
    i = pl.program_id(0)
    tm = x_ref.shape[0]
    b_cols, c_cols, h_cols = (slice(k * CONV_W, (k + 1) * CONV_W) for k in range(3))

    u = gates_ref[:, c_cols].astype(F32) * gates_ref[:, h_cols].astype(F32)
    last = CONV_HALO_ROWS - 1
    u_prev = (gprev_ref[last:, c_cols].astype(F32)
              * gprev_ref[last:, h_cols].astype(F32))
    u_next = gnext_ref[:1, c_cols].astype(F32) * gnext_ref[:1, h_cols].astype(F32)
    pos = i % tiles_per_seq
    u_prev = jnp.where(pos == 0, jnp.zeros_like(u_prev), u_prev)
    u_next = jnp.where(pos == tiles_per_seq - 1, jnp.zeros_like(u_next), u_next)
    row = lax.broadcasted_iota(jnp.int32, u.shape, 0)
    up = jnp.where(row == 0, u_prev, pltpu.roll(u, 1, axis=0))
    dn = jnp.where(row == tm - 1, u_next, pltpu.roll(u, tm - 1, axis=0))
    cw = cw_ref[...]
    y = up * cw[0:1, :] + u * cw[1:2, :] + dn * cw[2:3, :]
    cnv = gates_ref[:, b_cols].astype(F32) * y
    cnv_n = _rms(cnv, gcv_ref[...])

    l1, l2, l3 = l1_ref[...], l2_ref[...], l3_ref[...]
    mx = jnp.maximum(jnp.maximum(l1, l2), l3)
    e1, e2, e3 = jnp.exp(l1 - mx), jnp.exp(l2 - mx), jnp.exp(l3 - mx)
    den = e1 + e2 + e3

    def spread(w):
        hi = w.astype(BF16)
        lo = (w - hi.astype(F32)).astype(BF16)
        return jnp.dot(jnp.concatenate([hi, lo], axis=-1), ex_ref[...],
                       preferred_element_type=F32)

    def unpacked(o_ref):
        return jnp.concatenate([_unpack_bf16_pair(o_ref[p], idx)
                                for p in range(HEAD_PAIRS) for idx in range(2)], axis=-1)

    o3 = unpacked(o3_ref)
    att = (o3 + spread(e1 / den) * (unpacked(o1_ref) - o3)
           + spread(e2 / den) * (unpacked(o2_ref) - o3))
    att_n = _rms(att, ga_ref[...])

    mixed = jnp.concatenate([att_n.astype(BF16), cnv_n.astype(BF16)], axis=-1)
    mix = jnp.dot(mixed, wo_ref[...], preferred_element_type=F32)
    out_ref[...] = x_ref[...] + _rms(mix, gpost_ref[...])


def _mixer_out(x, gates, branches, conv_w, g_attn, g_conv, w_o, g_post, *, layer, seq_len, tm):
    t = x.shape[0]
    assert t % tm == 0 and seq_len % tm == 0 and tm % CONV_HALO_ROWS == 0
    tiles_per_seq = seq_len // tm
    halo_per_tile = tm // CONV_HALO_ROWS
    n_halo = t // CONV_HALO_ROWS

    def rows(width):
        return pl.BlockSpec((tm, width), lambda i: (i, 0))

    packed = pl.BlockSpec((None, HEAD_PAIRS, tm, V7X_LANES),
                          lambda i: (i // tiles_per_seq, 0, i % tiles_per_seq, 0))
    lse = pl.BlockSpec((None, tm, V7X_LANES),
                       lambda i: (i // tiles_per_seq, i % tiles_per_seq, 0))

    halo_prev = pl.BlockSpec((CONV_HALO_ROWS, GATES_W),
                             lambda i: (jnp.maximum(i * halo_per_tile - 1, 0), 0))
    halo_next = pl.BlockSpec((CONV_HALO_ROWS, GATES_W),
                             lambda i: (jnp.minimum((i + 1) * halo_per_tile, n_halo - 1), 0))

    def vec(width, nrows=1):
        return _layer_block(layer, (nrows, width), lambda i: (0, 0))

    head_of_col = jnp.arange(ATT_W, dtype=jnp.int32) // HEAD_DIM
    lane = jnp.arange(2 * V7X_LANES, dtype=jnp.int32) % V7X_LANES
    expand = (lane[:, None] == head_of_col[None, :]).astype(BF16)

    (o1, l1), (o2, l2), (o3, l3) = branches
    return pl.pallas_call(
        functools.partial(_mixer_out_kernel, tiles_per_seq=tiles_per_seq),
        grid=(t // tm,),
        in_specs=[packed, packed, packed, lse, lse, lse,
                  rows(GATES_W), halo_prev, halo_next,
                  vec(CONV_W, 3), vec(ATT_W), vec(CONV_W),
                  _layer_block(layer, (D_MODEL, D_MODEL), lambda i: (0, 0), resident=True),
                  vec(D_MODEL), rows(D_MODEL),
                  _resident((2 * V7X_LANES, ATT_W), lambda i: (0, 0))],
        out_specs=rows(D_MODEL),
        out_shape=jax.ShapeDtypeStruct((t, D_MODEL), F32),
        compiler_params=_params(("parallel",), 48),
        name="mixer_out",
    )(o1, o2, o3, l1, l2, l3, gates, gates, gates,
      conv_w, g_attn, g_conv, w_o, g_post, x, expand)


def _cross_attn_kernel(x_ref, gpre_ref, wq_ref, k_ref, v_ref, wo_ref, gpost_ref, out_ref):
    x = x_ref[...]
    xn = _rms(x, gpre_ref[...]).astype(BF16)
    q = jnp.dot(xn, wq_ref[...], preferred_element_type=F32).astype(BF16)
    scale = X_HEAD_DIM ** -0.5
    heads = []
    for h in range(X_HEADS):
        cols = slice(h * X_HEAD_DIM, (h + 1) * X_HEAD_DIM)
        s = lax.dot_general(q[:, cols], k_ref[:, cols], (((1,), (1,)), ((), ())),
                            preferred_element_type=F32) * scale
        m = jnp.max(s, axis=-1, keepdims=True)
        e = jnp.exp(s - m)
        p = e / jnp.sum(e, axis=-1, keepdims=True)
        o = jnp.dot(p.astype(BF16), v_ref[:, cols], preferred_element_type=F32)
        heads.append(o.astype(BF16))
    o = jnp.concatenate(heads, axis=-1)
    xa = jnp.dot(o, wo_ref[...], preferred_element_type=F32)
    out_ref[...] = x + _rms(xa, gpost_ref[...])


def _cross_attn(x, kv, g_pre, w_q, w_o, g_post, *, layer, seq_len, tm):
    t = x.shape[0]
    assert t % tm == 0 and seq_len % tm == 0
    tiles_per_seq = seq_len // tm
    vec = _layer_block(layer, (1, D_MODEL), lambda i: (0, 0))
    rows = pl.BlockSpec((tm, D_MODEL), lambda i: (i, 0))
    return pl.pallas_call(
        _cross_attn_kernel,
        grid=(t // tm,),
        in_specs=[rows, vec,
                  _layer_block(layer, (D_MODEL, D_MODEL), lambda i: (0, 0), resident=True),
                  pl.BlockSpec((N_MEM, D_MODEL), lambda i: (i // tiles_per_seq, 0)),
                  pl.BlockSpec((N_MEM, D_MODEL), lambda i: (i // tiles_per_seq, 1)),
                  _layer_block(layer, (D_MODEL, D_MODEL), lambda i: (0, 0), resident=True),
                  vec],
        out_specs=rows,
        out_shape=jax.ShapeDtypeStruct((t, D_MODEL), F32),
        compiler_params=_params(("parallel",), 56),
        name="cross_attn",
    )(x, g_pre, w_q, kv, kv, w_o, g_post)


def _ffn_kernel(x_ref, gpre_ref, wg_ref, wu_ref, wd_ref, gpost_ref, out_ref, h_ref):
    j = pl.program_id(1)
    last = pl.num_programs(1) - 1

    rc = FFN_ROW_CHUNK

    def body(first, final):
        for ch in range(x_ref.shape[0] // rc):
            rows = slice(ch * rc, (ch + 1) * rc)
            if first:
                h = _rms(x_ref[rows, :], gpre_ref[...]).astype(BF16)
                h_ref[rows, :] = h
            else:
                h = h_ref[rows, :]
            gate = jnp.dot(h, wg_ref[...], preferred_element_type=F32)
            up = jnp.dot(h, wu_ref[...], preferred_element_type=F32)
            act = (gate * jax.nn.sigmoid(gate) * up).astype(BF16)
            part = jnp.dot(act, wd_ref[...], preferred_element_type=F32)
            if first:
                out_ref[rows, :] = part
            elif final:
                acc = out_ref[rows, :] + part
                out_ref[rows, :] = x_ref[rows, :] + _rms(acc, gpost_ref[...])
            else:
                out_ref[rows, :] += part

    @pl.when(j == 0)
    def _():
        body(True, False)

    @pl.when((j > 0) & (j < last))
    def _():
        body(False, False)

    @pl.when(j == last)
    def _():
        body(False, True)


def _ffn(x, g_pre, w_gate, w_up, w_down, g_post, *, layer, tm, tf):
    t = x.shape[0]
    assert t % tm == 0 and D_FF % tf == 0 and D_FF // tf >= 2 and tm % FFN_ROW_CHUNK == 0
    vec = _layer_block(layer, (1, D_MODEL), lambda i, j: (0, 0))
    rows = pl.BlockSpec((tm, D_MODEL), lambda i, j: (i, 0))
    return pl.pallas_call(
        _ffn_kernel,
        grid=(t // tm, D_FF // tf),
        in_specs=[rows, vec,
                  _layer_block(layer, (None, D_MODEL, tf), lambda i, j: (j, 0, 0)),
                  _layer_block(layer, (None, D_MODEL, tf), lambda i, j: (j, 0, 0)),
                  _layer_block(layer, (tf, D_MODEL), lambda i, j: (j, 0)),
                  vec],
        out_specs=rows,
        out_shape=jax.ShapeDtypeStruct((t, D_MODEL), F32),
        scratch_shapes=[pltpu.VMEM((tm, D_MODEL), BF16)],
        compiler_params=_params(("parallel", "arbitrary"), 58),
        name="swiglu_ffn",
    )(x, g_pre, w_gate, w_up, w_down, g_post)


def _cast_tile_kernel(w_ref, o_ref):
    o_ref[...] = w_ref[...].astype(BF16)


def _cast_column_tiles(w, tn):
    depth, k, n = w.shape
    assert n % tn == 0
    return pl.pallas_call(
        _cast_tile_kernel,
        grid=(depth, n // tn),
        in_specs=[pl.BlockSpec((None, k, tn), lambda l, j: (l, 0, j))],
        out_specs=pl.BlockSpec((None, None, k, tn), lambda l, j: (l, j, 0, 0)),
        out_shape=jax.ShapeDtypeStruct((depth, n // tn, k, tn), BF16),
        compiler_params=_params(("parallel", "parallel"), 32),
        name="cast_column_tiles",
    )(w)


def _trunk(x, mem, params, rope):
    batch, seq_len, _ = x.shape
    x = x.reshape(batch * seq_len, D_MODEL)
    mem = mem.reshape(batch * N_MEM, D_MODEL)
    p = params
    for layer in range(p["w_in"].shape[0]):
        *zs, gates = _in_proj(x, p["g_mix_pre"], p["w_in"], rope, layer=layer, batch=batch,
                              seq_len=seq_len, tm=IN_PROJ_ROW_TILE, tn=IN_PROJ_COL_TILE)
        branches = [_dilated_branch(z, dil) for z, dil in zip(zs, DILATIONS)]
        x = _mixer_out(x, gates, branches, p["conv_w"], p["g_attn_out"], p["g_conv_out"],
                       p["w_o"], p["g_mix_post"], layer=layer, seq_len=seq_len, tm=ROW_TILE)
        kv = _norm_matmul(mem, p["g_mem"], p["w_xkv"], layer=layer,
                          tm=min(mem.shape[0], KV_ROW_TILE), tn=KV_COL_TILE)
        x = _cross_attn(x, kv, p["g_x_pre"], p["w_xq"], p["w_xo"], p["g_x_post"],
                        layer=layer, seq_len=seq_len, tm=ROW_TILE)
        x = _ffn(x, p["g_ffn_pre"], p["w_gate"], p["w_up"], p["w_down"], p["g_ffn_post"],
                 layer=layer, tm=FFN_ROW_TILE, tf=FFN_COL_TILE)
    return x.reshape(batch, seq_len, D_MODEL)


def kernel(x_prompt, x_sample, mem_prompt, mem_sample, g_mix_pre, w_in, conv_w, g_attn_out, g_conv_out, w_o, g_mix_post, g_x_pre, g_mem, w_xq, w_xk, w_xv, w_xo, g_x_post, g_ffn_pre, w_gate, w_up, w_down, g_ffn_post):
    gains = {"g_mix_pre": g_mix_pre, "g_attn_out": g_attn_out, "g_conv_out": g_conv_out,
             "g_mix_post": g_mix_post, "g_x_pre": g_x_pre, "g_mem": g_mem,
             "g_x_post": g_x_post, "g_ffn_pre": g_ffn_pre, "g_ffn_post": g_ffn_post}
    params = {name: g[:, None, :] for name, g in gains.items()}
    params.update({
        "conv_w": conv_w,
        "w_in": w_in.astype(BF16), "w_o": w_o.astype(BF16), "w_xq": w_xq.astype(BF16),
        "w_xkv": jnp.concatenate([w_xk, w_xv], axis=2).astype(BF16),
        "w_xo": w_xo.astype(BF16), "w_gate": _cast_column_tiles(w_gate, FFN_COL_TILE),
        "w_up": _cast_column_tiles(w_up, FFN_COL_TILE), "w_down": w_down.astype(BF16),
    })
    outs = []
    for x, mem in ((x_prompt, mem_prompt), (x_sample, mem_sample)):
        rope = _rope_tables(x.shape[1])
        outs.append(_trunk(x, mem, params, rope))
    return tuple(outs)
```

```python
import functools

import jax
import jax.numpy as jnp
from jax import lax
from jax.experimental import pallas as pl
from jax.experimental.pallas import tpu as pltpu

F32 = jnp.float32
BF16 = jnp.bfloat16

D_MODEL = 2048
HEAD_DIM = 128
N_ATT_HEADS = 12
ATT_W = N_ATT_HEADS * HEAD_DIM
QKV_W = 3 * ATT_W
CONV_W = D_MODEL - ATT_W
GATES_W = 3 * CONV_W
DILATIONS = (1, 4, 16)
BAND_HALF = 64
ROT_DIM = HEAD_DIM // 4
ROPE_THETA = 500000.0
N_MEM = 256
X_HEADS = 4
X_HEAD_DIM = D_MODEL // X_HEADS
D_FF = 5632
EPS = 1e-6
NEG = -1e30
LOG2_E = 1.4426950408889634

V7X_LANES = 128
V7X_BF16_SUBLANES = 16
V7X_VMEM_BYTES = 64 * 1024 * 1024
MIB = 1024 * 1024

ROW_TILE = 512
IN_PROJ_ROW_TILE = 256
IN_PROJ_COL_TILE = ATT_W
IN_PROJ_SORT_RING = 12
FFN_ROW_TILE = 1024
KV_ROW_TILE = 1024
KV_COL_TILE = 1024
FFN_COL_TILE = 512
FFN_ROW_CHUNK = 512
ATT_Q_BLOCK = 128
ATT_MAX_Q_TILE = 1024
CONV_HALO_ROWS = V7X_BF16_SUBLANES
HEAD_PAIRS = N_ATT_HEADS // 2


def _params(semantics, vmem_mib):
    assert vmem_mib * MIB < V7X_VMEM_BYTES
    return pltpu.CompilerParams(dimension_semantics=semantics,
                                vmem_limit_bytes=vmem_mib * MIB)


def _resident(shape, index_map):
    return pl.BlockSpec(shape, index_map, pipeline_mode=pl.Buffered(1))


def _layer_block(layer, block, index_map, resident=False):
    def stacked_index(*grid_idx):
        return (layer,) + tuple(index_map(*grid_idx))
    if resident:
        return _resident((None,) + tuple(block), stacked_index)
    return pl.BlockSpec((None,) + tuple(block), stacked_index)


def _rms(x, g):
    ms = jnp.mean(x * x, axis=-1, keepdims=True)
    return (x * lax.rsqrt(ms + EPS)) * g


def _pack_bf16_pair(a, b):
    lo = lax.bitcast_convert_type(a.astype(BF16).astype(F32), jnp.uint32) >> 16
    hi = lax.bitcast_convert_type(b.astype(BF16).astype(F32), jnp.uint32) & jnp.uint32(0xFFFF0000)
    return lo | hi


def _unpack_bf16_pair(word, index):
    bits = (word << 16) if index == 0 else (word & jnp.uint32(0xFFFF0000))
    return lax.bitcast_convert_type(bits, F32)


def _in_proj_kernel(x_ref, g_ref, w_ref, cos_ref, sin_ref, *rest, tn):
    z1_refs, z4_refs, z16_refs = rest[0:3], rest[3:6], rest[6:9]
    gates_ref, h_ref, slab_ref, slab4_ref = rest[9:]
    tm = x_ref.shape[0]
    h_ref[...] = _rms(x_ref[...], g_ref[...]).astype(BF16)
    cos = cos_ref[...]
    sin = sin_ref[...]
    lane = lax.broadcasted_iota(jnp.int32, cos.shape, 1)
    first_half = lane < ROT_DIM // 2

    def project(col0):
        return jnp.dot(h_ref[...], w_ref[:, col0:col0 + tn], preferred_element_type=F32)

    for col0 in range(0, QKV_W, tn):
        acc = project(col0)
        for c in range(tn // V7X_LANES):
            lo_col = col0 + c * V7X_LANES
            part, part_col = divmod(lo_col, ATT_W)
            z1_ref, z4_ref, z16_ref = z1_refs[part], z4_refs[part], z16_refs[part]
            cols = slice(part_col, part_col + V7X_LANES)
            slab = lo_col // V7X_LANES
            t = acc[:, c * V7X_LANES:(c + 1) * V7X_LANES]
            if lo_col < 2 * ATT_W:
                hi = pltpu.roll(t, HEAD_DIM - ROT_DIM // 2, axis=1)
                lo = pltpu.roll(t, ROT_DIM // 2, axis=1)
                t = t * cos + jnp.where(first_half, hi, lo) * sin
            z1_ref[:, cols] = t.astype(BF16)
            slab_ref[slab] = t
            n4, n16 = tm // 4, tm // 16
            ring = slab % slab4_ref.shape[0]
            for r in range(4):
                s4 = slab_ref[slab, pl.ds(r, n4, stride=4), :]
                z4_ref[r, :, cols] = s4.astype(BF16)
                slab4_ref[ring, r * n4:(r + 1) * n4, :] = s4
            for r in range(4):
                for t4 in range(4):
                    s16 = slab4_ref[ring, pl.ds(r * n4 + t4, n16, stride=4), :]
                    z16_ref[r + 4 * t4, :, cols] = s16.astype(BF16)
    for col0 in range(0, GATES_W, tn):
        gates_ref[:, col0:col0 + tn] = project(QKV_W + col0).astype(BF16)


def _in_proj(x, g, w, rope, *, layer, batch, seq_len, tm, tn):
    t, d = x.shape
    n = w.shape[-1]
    assert n == QKV_W + GATES_W and t % tm == 0 and seq_len % tm == 0
    assert QKV_W % tn == 0 and GATES_W % tn == 0 and tn % V7X_LANES == 0
    assert tm % (16 * V7X_BF16_SUBLANES) == 0
    tiles_per_seq = seq_len // tm

    def sorted_spec(dil):
        return pl.BlockSpec((None, dil, tm // dil, ATT_W),
                            lambda i: (i // tiles_per_seq, 0, i % tiles_per_seq, 0))

    def sorted_shape(dil):
        return jax.ShapeDtypeStruct((batch, dil, seq_len // dil, ATT_W), BF16)

    rope_spec = pl.BlockSpec((tm, HEAD_DIM), lambda i: (i % tiles_per_seq, 0))
    *zs, gates = pl.pallas_call(
        functools.partial(_in_proj_kernel, tn=tn),
        grid=(t // tm,),
        in_specs=[pl.BlockSpec((tm, d), lambda i: (i, 0)),
                  _layer_block(layer, (1, d), lambda i: (0, 0)),
                  _layer_block(layer, (d, n), lambda i: (0, 0), resident=True),
                  rope_spec, rope_spec],
        out_specs=([pl.BlockSpec((tm, ATT_W), lambda i: (i, 0))] * 3
                   + [sorted_spec(4)] * 3 + [sorted_spec(16)] * 3
                   + [pl.BlockSpec((tm, GATES_W), lambda i: (i, 0))]),
        out_shape=([jax.ShapeDtypeStruct((t, ATT_W), BF16)] * 3
                   + [sorted_shape(4)] * 3 + [sorted_shape(16)] * 3
                   + [jax.ShapeDtypeStruct((t, GATES_W), BF16)]),
        scratch_shapes=[pltpu.VMEM((tm, d), BF16),
                        pltpu.VMEM((QKV_W // V7X_LANES, tm, V7X_LANES), F32),
                        pltpu.VMEM((IN_PROJ_SORT_RING, tm, V7X_LANES), F32)],
        compiler_params=_params(("parallel",), 58),
        name="in_proj",
    )(x, g, w, *rope)
    natural = tuple(z.reshape(batch, 1, seq_len, ATT_W) for z in zs[0:3])
    return natural, tuple(zs[3:6]), tuple(zs[6:9]), gates


def _rope_tables(seq_len):
    inv = jnp.float32(ROPE_THETA) ** (-jnp.arange(0, ROT_DIM, 2, dtype=F32) / ROT_DIM)
    ang = jnp.arange(seq_len, dtype=F32)[:, None] * inv[None, :]
    cos, sin = jnp.cos(ang), jnp.sin(ang)
    pad = HEAD_DIM - ROT_DIM
    cos_t = jnp.concatenate([cos, cos, jnp.ones((seq_len, pad), F32)], axis=1)
    sin_t = jnp.concatenate([-sin, sin, jnp.zeros((seq_len, pad), F32)], axis=1)
    return cos_t, sin_t


def _norm_matmul_kernel(x_ref, g_ref, w_ref, o_ref, h_ref):
    @pl.when(pl.program_id(1) == 0)
    def _():
        h_ref[...] = _rms(x_ref[...], g_ref[...]).astype(BF16)

    o_ref[...] = jnp.dot(h_ref[...], w_ref[...], preferred_element_type=F32).astype(BF16)


def _norm_matmul(x, g, w, *, layer, tm, tn):
    rows, d = x.shape
    n = w.shape[-1]
    assert rows % tm == 0 and n % tn == 0
    return pl.pallas_call(
        _norm_matmul_kernel,
        grid=(rows // tm, n // tn),
        in_specs=[pl.BlockSpec((tm, d), lambda i, j: (i, 0)),
                  _layer_block(layer, (1, d), lambda i, j: (0, 0)),
                  _layer_block(layer, (d, tn), lambda i, j: (0, j))],
        out_specs=pl.BlockSpec((tm, tn), lambda i, j: (i, j)),
        out_shape=jax.ShapeDtypeStruct((rows, n), BF16),
        scratch_shapes=[pltpu.VMEM((tm, d), BF16)],
        compiler_params=_params(("parallel", "arbitrary"), 40),
        name="norm_matmul",
    )(x, g, w)


def _branch_kernel(q_ref, kp_ref, k_ref, kn_ref, vp_ref, v_ref, vn_ref, o_ref, lse_ref,
                   *, res_len, dil):
    rs, tq, _ = q_ref.shape
    base = pl.program_id(1) * tq
    tb = ATT_Q_BLOCK
    nk = tb + 2 * BAND_HALF
    qi = lax.broadcasted_iota(jnp.int32, (tb, nk), 0)
    kj = lax.broadcasted_iota(jnp.int32, (tb, nk), 1)
    off = kj - qi
    band = (off >= 0) & (off <= 2 * BAND_HALF)
    lane = lax.broadcasted_iota(jnp.int32, (tb, V7X_LANES), 1)
    scale = HEAD_DIM ** -0.5
    exp2_scale = scale * LOG2_E
    ones = jnp.ones((nk, HEAD_DIM), BF16)
    for rr in range(rs):
        r = pl.program_id(2) * rs + rr
        keys = jnp.concatenate([kp_ref[rr], k_ref[rr], kn_ref[rr]], axis=0)
        vals = jnp.concatenate([vp_ref[rr], v_ref[rr], vn_ref[rr]], axis=0)
        for blk in range(tq // tb):
            kpos = base + blk * tb - BAND_HALF + kj
            valid = band & (kpos >= 0) & (kpos < res_len)
            qrows = slice(blk * tb, (blk + 1) * tb)
            krows = slice(blk * tb, blk * tb + nk)
            if dil == 1:
                out_rows = pl.ds(blk * tb, tb)
            else:
                out_rows = pl.ds(r + dil * blk * tb, tb, stride=dil)
            m_all = jnp.zeros((tb, V7X_LANES), F32)
            l_all = jnp.ones((tb, V7X_LANES), F32)
            for p in range(HEAD_PAIRS):
                pair = []
                for h in (2 * p, 2 * p + 1):
                    cols = slice(h * HEAD_DIM, (h + 1) * HEAD_DIM)
                    s = lax.dot_general(q_ref[rr, qrows, cols], keys[krows, cols],
                                        (((1,), (1,)), ((), ())),
                                        preferred_element_type=F32)
                    s = jnp.where(valid, s, NEG)
                    m = jnp.max(s, axis=-1, keepdims=True)
                    e = jnp.exp2((s - m) * exp2_scale)
                    v_ones = jnp.concatenate([vals[krows, cols], ones], axis=-1)
                    ol = jnp.dot(e.astype(BF16), v_ones, preferred_element_type=F32)
                    l = ol[:, HEAD_DIM:]
                    pair.append(ol[:, :HEAD_DIM] / l)
                    m_all = jnp.where(lane == h, m, m_all)
                    l_all = jnp.where(lane == h, l, l_all)
                o_ref[p, out_rows, :] = _pack_bf16_pair(*pair)
            lse_ref[out_rows, :] = m_all * scale + jnp.log(l_all)


def _dilated_branch(qkv, dil):
    q, k, v = qkv
    batch, _, res_len, _ = q.shape
    seq_len = res_len * dil
    tq = min(ATT_MAX_Q_TILE, res_len)
    rs = min(dil, ATT_MAX_Q_TILE // tq)
    assert res_len % tq == 0 and tq % ATT_Q_BLOCK == 0 and tq % BAND_HALF == 0
    assert dil % rs == 0
    halo_per_tile = tq // BAND_HALF
    n_halo_blocks = res_len // BAND_HALF

    main = pl.BlockSpec((None, rs, tq, ATT_W), lambda b, c, g: (b, g, c, 0))
    prev = pl.BlockSpec((None, rs, BAND_HALF, ATT_W),
                        lambda b, c, g: (b, g, jnp.maximum(c * halo_per_tile - 1, 0), 0))
    nxt = pl.BlockSpec((None, rs, BAND_HALF, ATT_W),
                       lambda b, c, g: (b, g, jnp.minimum((c + 1) * halo_per_tile,
                                                          n_halo_blocks - 1), 0))

    return pl.pallas_call(
        functools.partial(_branch_kernel, res_len=res_len, dil=dil),
        grid=(batch, res_len // tq, dil // rs),
        in_specs=[main, prev, main, nxt, prev, main, nxt],
        out_specs=[pl.BlockSpec((None, HEAD_PAIRS, tq * dil, V7X_LANES),
                                lambda b, c, g: (b, 0, c, 0)),
                   pl.BlockSpec((None, tq * dil, V7X_LANES), lambda b, c, g: (b, c, 0))],
        out_shape=[jax.ShapeDtypeStruct((batch, HEAD_PAIRS, seq_len, V7X_LANES), jnp.uint32),
                   jax.ShapeDtypeStruct((batch, seq_len, V7X_LANES), F32)],
        compiler_params=_params(("parallel", "parallel", "arbitrary"), 56),
        name=f"dilated_branch_d{dil}",
    )(q, k, k, k, v, v, v)


def _mixer_out_kernel(o1_ref, o2_ref, o3_ref, l1_ref, l2_ref, l3_ref,
                      gates_ref, gprev_ref, gnext_ref,
                      cw_ref, ga_ref, gcv_ref, wo_ref, gpost_ref, x_ref, ex_ref, out_ref,
                      *, tiles_per_seq):
    i = pl.program_id(0)
    tm = x_ref.shape[0]
    b_cols, c_cols, h_cols = (slice(k * CONV_W, (k + 1) * CONV_W) for k in range(3))

    u = gates_ref[:, c_cols].astype(F32) * gates_ref[:, h_cols].astype(F32)
    last = CONV_HALO_ROWS - 1
    u_prev = (gprev_ref[last:, c_cols].astype(F32)
              * gprev_ref[last:, h_cols].astype(F32))
    u_next = gnext_ref[:1, c_cols].astype(F32) * gnext_ref[:1, h_cols].astype(F32)
    pos = i % tiles_per_seq
    u_prev = jnp.where(pos == 0, jnp.zeros_like(u_prev), u_prev)
    u_next = jnp.where(pos == tiles_per_seq - 1, jnp.zeros_like(u_next), u_next)
    row = lax.broadcasted_iota(jnp.int32, u.shape, 0)
    up = jnp.where(row == 0, u_prev, pltpu.roll(u, 1, axis=0))
    dn = jnp.where(row == tm - 1, u_next, pltpu.roll(u, tm - 1, axis=0))
    cw = cw_ref[...]
    y = up * cw[0:1, :] + u * cw[1:2, :] + dn * cw[2:3, :]
    cnv = gates_ref[:, b_cols].astype(F32) * y
    cnv_n = _rms(cnv, gcv_ref[...])

    l1, l2, l3 = l1_ref[...], l2_ref[...], l3_ref[...]
    mx = jnp.maximum(jnp.maximum(l1, l2), l3)
    e1, e2, e3 = jnp.exp(l1 - mx), jnp.exp(l2 - mx), jnp.exp(l3 - mx)
    den = e1 + e2 + e3

    def spread(w):
        hi = w.astype(BF16)
        lo = (w - hi.astype(F32)).astype(BF16)
        return jnp.dot(jnp.concatenate([hi, lo], axis=-1), ex_ref[...],
                       preferred_element_type=F32)

    def unpacked(o_ref):
        return jnp.concatenate([_unpack_bf16_pair(o_ref[p], idx)
                                for p in range(HEAD_PAIRS) for idx in range(2)], axis=-1)

    o3 = unpacked(o3_ref)
    att = (o3 + spread(e1 / den) * (unpacked(o1_ref) - o3)
           + spread(e2 / den) * (unpacked(o2_ref) - o3))
    att_n = _rms(att, ga_ref[...])

    mixed = jnp.concatenate([att_n.astype(BF16), cnv_n.astype(BF16)], axis=-1)
    mix = jnp.dot(mixed, wo_ref[...], preferred_element_type=F32)
    out_ref[...] = x_ref[...] + _rms(mix, gpost_ref[...])


def _mixer_out(x, gates, branches, conv_w, g_attn, g_conv, w_o, g_post, *, layer, seq_len, tm):
    t = x.shape[0]
    assert t % tm == 0 and seq_len % tm == 0 and tm % CONV_HALO_ROWS == 0
    tiles_per_seq = seq_len // tm
    halo_per_tile = tm // CONV_HALO_ROWS
    n_halo = t // CONV_HALO_ROWS

    def rows(width):
        return pl.BlockSpec((tm, width), lambda i: (i, 0))

    packed = pl.BlockSpec((None, HEAD_PAIRS, tm, V7X_LANES),
                          lambda i: (i // tiles_per_seq, 0, i % tiles_per_seq, 0))
    lse = pl.BlockSpec((None, tm, V7X_LANES),
                       lambda i: (i // tiles_per_seq, i % tiles_per_seq, 0))

    halo_prev = pl.BlockSpec((CONV_HALO_ROWS, GATES_W),
                             lambda i: (jnp.maximum(i * halo_per_tile - 1, 0), 0))
    halo_next = pl.BlockSpec((CONV_HALO_ROWS, GATES_W),
                             lambda i: (jnp.minimum((i + 1) * halo_per_tile, n_halo - 1), 0))

    def vec(width, nrows=1):
        return _layer_block(layer, (nrows, width), lambda i: (0, 0))

    head_of_col = jnp.arange(ATT_W, dtype=jnp.int32) // HEAD_DIM
    lane = jnp.arange(2 * V7X_LANES, dtype=jnp.int32) % V7X_LANES
    expand = (lane[:, None] == head_of_col[None, :]).astype(BF16)

    (o1, l1), (o2, l2), (o3, l3) = branches
    return pl.pallas_call(
        functools.partial(_mixer_out_kernel, tiles_per_seq=tiles_per_seq),
        grid=(t // tm,),
        in_specs=[packed, packed, packed, lse, lse, lse,
                  rows(GATES_W), halo_prev, halo_next,
                  vec(CONV_W, 3), vec(ATT_W), vec(CONV_W),
                  _layer_block(layer, (D_MODEL, D_MODEL), lambda i: (0, 0), resident=True),
                  vec(D_MODEL), rows(D_MODEL),
                  _resident((2 * V7X_LANES, ATT_W), lambda i: (0, 0))],
        out_specs=rows(D_MODEL),
        out_shape=jax.ShapeDtypeStruct((t, D_MODEL), F32),
        compiler_params=_params(("parallel",), 48),
        name="mixer_out",
    )(o1, o2, o3, l1, l2, l3, gates, gates, gates,
      conv_w, g_attn, g_conv, w_o, g_post, x, expand)


def _cross_attn_kernel(x_ref, gpre_ref, wq_ref, k_ref, v_ref, wo_ref, gpost_ref, out_ref):
    x = x_ref[...]
    xn = _rms(x, gpre_ref[...]).astype(BF16)
    q = jnp.dot(xn, wq_ref[...], preferred_element_type=F32).astype(BF16)
    scale = X_HEAD_DIM ** -0.5
    heads = []
    for h in range(X_HEADS):
        cols = slice(h * X_HEAD_DIM, (h + 1) * X_HEAD_DIM)
        s = lax.dot_general(q[:, cols], k_ref[:, cols], (((1,), (1,)), ((), ())),
                            preferred_element_type=F32) * scale
        m = jnp.max(s, axis=-1, keepdims=True)
        e = jnp.exp(s - m)
        p = e / jnp.sum(e, axis=-1, keepdims=True)
        o = jnp.dot(p.astype(BF16), v_ref[:, cols], preferred_element_type=F32)
        heads.append(o.astype(BF16))
    o = jnp.concatenate(heads, axis=-1)
    xa = jnp.dot(o, wo_ref[...], preferred_element_type=F32)
    out_ref[...] = x + _rms(xa, gpost_ref[...])


def _cross_attn(x, kv, g_pre, w_q, w_o, g_post, *, layer, seq_len, tm):
    t = x.shape[0]
    assert t % tm == 0 and seq_len % tm == 0
    tiles_per_seq = seq_len // tm
    vec = _layer_block(layer, (1, D_MODEL), lambda i: (0, 0))
    rows = pl.BlockSpec((tm, D_MODEL), lambda i: (i, 0))
    return pl.pallas_call(
        _cross_attn_kernel,
        grid=(t // tm,),
        in_specs=[rows, vec,
                  _layer_block(layer, (D_MODEL, D_MODEL), lambda i: (0, 0), resident=True),
                  pl.BlockSpec((N_MEM, D_MODEL), lambda i: (i // tiles_per_seq, 0)),
                  pl.BlockSpec((N_MEM, D_MODEL), lambda i: (i // tiles_per_seq, 1)),
                  _layer_block(layer, (D_MODEL, D_MODEL), lambda i: (0, 0), resident=True),
                  vec],
        out_specs=rows,
        out_shape=jax.ShapeDtypeStruct((t, D_MODEL), F32),
        compiler_params=_params(("parallel",), 56),
        name="cross_attn",
    )(x, g_pre, w_q, kv, kv, w_o, g_post)


def _ffn_kernel(x_ref, gpre_ref, wg_ref, wu_ref, wd_ref, gpost_ref, out_ref, h_ref):
    j = pl.program_id(1)
    last = pl.num_programs(1) - 1

    rc = FFN_ROW_CHUNK

    def body(first, final):
        for ch in range(x_ref.shape[0] // rc):
            rows = slice(ch * rc, (ch + 1) * rc)
            if first:
                h = _rms(x_ref[rows, :], gpre_ref[...]).astype(BF16)
                h_ref[rows, :] = h
            else:
                h = h_ref[rows, :]
            gate = jnp.dot(h, wg_ref[...], preferred_element_type=F32)
            up = jnp.dot(h, wu_ref[...], preferred_element_type=F32)
            act = (gate * jax.nn.sigmoid(gate) * up).astype(BF16)
            part = jnp.dot(act, wd_ref[...], preferred_element_type=F32)
            if first:
                out_ref[rows, :] = part
            elif final:
                acc = out_ref[rows, :] + part
                out_ref[rows, :] = x_ref[rows, :] + _rms(acc, gpost_ref[...])
            else:
                out_ref[rows, :] += part

    @pl.when(j == 0)
    def _():
        body(True, False)

    @pl.when((j > 0) & (j < last))
    def _():
        body(False, False)

    @pl.when(j == last)
    def _():
        body(False, True)


def _ffn(x, g_pre, w_gate, w_up, w_down, g_post, *, layer, tm, tf):
    t = x.shape[0]
    assert t % tm == 0 and D_FF % tf == 0 and D_FF // tf >= 2 and tm % FFN_ROW_CHUNK == 0
    vec = _layer_block(layer, (1, D_MODEL), lambda i, j: (0, 0))
    rows = pl.BlockSpec((tm, D_MODEL), lambda i, j: (i, 0))
    return pl.pallas_call(
        _ffn_kernel,
        grid=(t // tm, D_FF // tf),
        in_specs=[rows, vec,
                  _layer_block(layer, (None, D_MODEL, tf), lambda i, j: (j, 0, 0)),
                  _layer_block(layer, (None, D_MODEL, tf), lambda i, j: (j, 0, 0)),
                  _layer_block(layer, (tf, D_MODEL), lambda i, j: (j, 0)),
                  vec],
        out_specs=rows,
        out_shape=jax.ShapeDtypeStruct((t, D_MODEL), F32),
        scratch_shapes=[pltpu.VMEM((tm, D_MODEL), BF16)],
        compiler_params=_params(("parallel", "arbitrary"), 58),
        name="swiglu_ffn",
    )(x, g_pre, w_gate, w_up, w_down, g_post)


def _cast_tile_kernel(w_ref, o_ref):
    o_ref[...] = w_ref[...].astype(BF16)


def _cast_column_tiles(w, tn):
    depth, k, n = w.shape
    assert n % tn == 0
    return pl.pallas_call(
        _cast_tile_kernel,
        grid=(depth, n // tn),
        in_specs=[pl.BlockSpec((None, k, tn), lambda l, j: (l, 0, j))],
        out_specs=pl.BlockSpec((None, None, k, tn), lambda l, j: (l, j, 0, 0)),
        out_shape=jax.ShapeDtypeStruct((depth, n // tn, k, tn), BF16),
        compiler_params=_params(("parallel", "parallel"), 32),
        name="cast_column_tiles",
    )(w)


def _trunk(x, mem, params, rope):
    batch, seq_len, _ = x.shape
    x = x.reshape(batch * seq_len, D_MODEL)
    mem = mem.reshape(batch * N_MEM, D_MODEL)
    p = params
    for layer in range(p["w_in"].shape[0]):
        *zs, gates = _in_proj(x, p["g_mix_pre"], p["w_in"], rope, layer=layer, batch=batch,
                              seq_len=seq_len, tm=IN_PROJ_ROW_TILE, tn=IN_PROJ_COL_TILE)
        branches = [_dilated_branch(z, dil) for z, dil in zip(zs, DILATIONS)]
        x = _mixer_out(x, gates, branches, p["conv_w"], p["g_attn_out"], p["g_conv_out"],
                       p["w_o"], p["g_mix_post"], layer=layer, seq_len=seq_len, tm=ROW_TILE)
        kv = _norm_matmul(mem, p["g_mem"], p["w_xkv"], layer=layer,
                          tm=min(mem.shape[0], KV_ROW_TILE), tn=KV_COL_TILE)
        x = _cross_attn(x, kv, p["g_x_pre"], p["w_xq"], p["w_xo"], p["g_x_post"],
                        layer=layer, seq_len=seq_len, tm=ROW_TILE)
        x = _ffn(x, p["g_ffn_pre"], p["w_gate"], p["w_up"], p["w_down"], p["g_ffn_post"],
                 layer=layer, tm=FFN_ROW_TILE, tf=FFN_COL_TILE)
    return x.reshape(batch, seq_len, D_MODEL)


def kernel(x_prompt, x_sample, mem_prompt, mem_sample, g_mix_pre, w_in, conv_w, g_attn_out, g_conv_out, w_o, g_mix_post, g_x_pre, g_mem, w_xq, w_xk, w_xv, w_xo, g_x_post, g_ffn_pre, w_gate, w_up, w_down, g_ffn_post):
    gains = {"g_mix_pre": g_mix_pre, "g_attn_out": g_attn_out, "g_conv_out": g_conv_out,
             "g_mix_post": g_mix_post, "g_x_pre": g_x_pre, "g_mem": g_mem,
             "g_x_post": g_x_post, "g_ffn_pre": g_ffn_pre, "g_ffn_post": g_ffn_post}
    params = {name: g[:, None, :] for name, g in gains.items()}
    params.update({
        "conv_w": conv_w,
        "w_in": w_in.astype(BF16), "w_o": w_o.astype(BF16), "w_xq": w_xq.astype(BF16),
        "w_xkv": jnp.concatenate([w_xk, w_xv], axis=2).astype(BF16),
        "w_xo": w_xo.astype(BF16), "w_gate": _cast_column_tiles(w_gate, FFN_COL_TILE),
        "w_up": _cast_column_tiles(w_up, FFN_COL_TILE), "w_down": w_down.astype(BF16),
    })
    outs = []
    for x, mem in ((x_prompt, mem_prompt), (x_sample, mem_sample)):
        rope = _rope_tables(x.shape[1])
        outs.append(_trunk(x, mem, params, rope))
    return tuple(outs)
```
